```python
import math
import jax, jax.numpy as jnp
from jax import lax
import numpy as np

D_MODEL = 1024
BATCH = 32
SEQ = 256
DEPTH = 2
DEC_BATCH = 8
DEC_SEQ = 2048
PAST_LEN = 512

GRID_W = 64
CHUNK = 128
GMLP_WIDTH = 1024
GMLP_GROUPS = 8
GMLP_GROUP_DIM = GMLP_WIDTH // GMLP_GROUPS
SSD_INNER = 2 * D_MODEL
SSD_HEADDIM = 64
SSD_HEADS = SSD_INNER // SSD_HEADDIM
SSD_GROUPS = 4
SSD_HPG = SSD_HEADS // SSD_GROUPS
SSD_STATE = 128
SSD_CONV = 5
SSD_CONV_CH = SSD_INNER + 2 * SSD_GROUPS * SSD_STATE
N_EXPERTS = 16
N_EXPERT_GROUPS = 4
EXPERTS_PER_GROUP = N_EXPERTS // N_EXPERT_GROUPS
TOP_K = 2
D_EXPERT = 512
N_MOD = 6
EPS = 1e-6

IN_PROJ_DIM = 2 * D_MODEL + 2 * GMLP_WIDTH + SSD_INNER + SSD_CONV_CH + 2 * SSD_HEADS
IN_SPLITS = (
    D_MODEL,
    2 * D_MODEL,
    2 * D_MODEL + GMLP_WIDTH,
    2 * D_MODEL + 2 * GMLP_WIDTH,
    2 * D_MODEL + 2 * GMLP_WIDTH + SSD_INNER,
    2 * D_MODEL + 2 * GMLP_WIDTH + SSD_INNER + SSD_CONV_CH,
    2 * D_MODEL + 2 * GMLP_WIDTH + SSD_INNER + SSD_CONV_CH + SSD_HEADS,
)

kernel_name = 'hybrid_gmlp_ssd_moe_diffusion_step'


def _rmsnorm(x, w):
    xf = x.astype(jnp.float32)
    y = xf * lax.rsqrt(jnp.mean(xf * xf, axis=-1, keepdims=True) + EPS)
    return (y * w.astype(jnp.float32)).astype(x.dtype)


def _group_rmsnorm(x, w, groups):
    shp = x.shape
    xg = x.astype(jnp.float32).reshape(shp[:-1] + (groups, shp[-1] // groups))
    y = xg * lax.rsqrt(jnp.mean(xg * xg, axis=-1, keepdims=True) + EPS)
    return (y.reshape(shp) * w.astype(jnp.float32)).astype(x.dtype)


def _grid_pos_embed(n_tokens, dim, dtype):
    rows = n_tokens // GRID_W
    quarter = dim // 4
    omega = 1.0 / (10000.0 ** (jnp.arange(quarter, dtype=jnp.float32) / quarter))
    r = jnp.arange(rows, dtype=jnp.float32)[:, None] * omega
    col = jnp.arange(GRID_W, dtype=jnp.float32)[:, None] * omega
    r_emb = jnp.concatenate([jnp.sin(r), jnp.cos(r)], axis=-1)
    c_emb = jnp.concatenate([jnp.sin(col), jnp.cos(col)], axis=-1)
    emb = jnp.concatenate([
        jnp.broadcast_to(r_emb[:, None, :], (rows, GRID_W, dim // 2)),
        jnp.broadcast_to(c_emb[None, :, :], (rows, GRID_W, dim // 2))], axis=-1)
    return emb.reshape(rows * GRID_W, dim).astype(dtype)


def _modulation(cond, w_mod, b_mod):
    return jax.nn.silu(cond) @ w_mod + b_mod


def _dwconv_centered(x, w, b):
    ch = x.shape[-1]
    pad = SSD_CONV // 2
    y = lax.conv_general_dilated(
        x, w[:, None, :].astype(x.dtype), window_strides=(1,), padding=[(pad, pad)],
        dimension_numbers=('NWC', 'WIO', 'NWC'), feature_group_count=ch)
    return y + b.astype(x.dtype)


def _flip_seq(t):
    return jnp.flip(t, axis=1)


def _ssd_chunked(x, dt, a_neg, bm, cm, h0):
    f32 = jnp.float32
    b, l = x.shape[:2]
    nc = l // CHUNK
    x = x.astype(f32)
    dt = dt.astype(f32)
    a = (dt * a_neg).reshape(b, nc, CHUNK, SSD_GROUPS, SSD_HPG)
    xdt = (x * dt[..., None]).reshape(b, nc, CHUNK, SSD_GROUPS, SSD_HPG, SSD_HEADDIM)
    bc = bm.astype(f32).reshape(b, nc, CHUNK, SSD_GROUPS, SSD_STATE)
    cc = cm.astype(f32).reshape(b, nc, CHUNK, SSD_GROUPS, SSD_STATE)
    a_cs = jnp.cumsum(a, axis=2)
    lower = jnp.tril(jnp.ones((CHUNK, CHUNK), dtype=bool))[:, :, None, None]
    seg = a_cs[:, :, :, None] - a_cs[:, :, None, :]
    decay = jnp.exp(jnp.where(lower, seg, -jnp.inf))
    cb = jnp.einsum('bctgn,bcsgn->bctsg', cc, bc)
    y_diag = jnp.einsum('bctsgh,bcsghp->bctghp', cb[..., None] * decay, xdt)
    decay_end = jnp.exp(a_cs[:, :, -1:] - a_cs)
    states = jnp.einsum('bcsgn,bcsghp->bcghpn', bc, xdt * decay_end[..., None])
    chunk_decay = jnp.exp(a_cs[:, :, -1])

    def step(h, inp):
        st, dec = inp
        return h * dec[..., None, None] + st, h

    h_last, h_in = lax.scan(step, h0.astype(f32),
                            (jnp.moveaxis(states, 1, 0), jnp.moveaxis(chunk_decay, 1, 0)))
    h_in = jnp.moveaxis(h_in, 0, 1)
    y_off = jnp.einsum('bctgn,bcghpn->bctghp', cc, h_in) * jnp.exp(a_cs)[..., None]
    y = (y_diag + y_off).reshape(b, l, SSD_GROUPS, SSD_HPG, SSD_HEADDIM)
    return y, h_last


def _bi_ssd(xbc, z, dt_f, dt_b, conv_w, conv_b, dt_bias, a_log, d_skip, norm_w, h0_f, h0_b):
    f32 = jnp.float32
    b, l, _ = xbc.shape
    xbc = jax.nn.silu(_dwconv_centered(xbc, conv_w, conv_b))
    xh, bm, cm = jnp.split(xbc, [SSD_INNER, SSD_INNER + SSD_GROUPS * SSD_STATE], axis=-1)
    xh = xh.reshape(b, l, SSD_GROUPS, SSD_HPG, SSD_HEADDIM)
    bm = bm.reshape(b, l, SSD_GROUPS, SSD_STATE)
    cm = cm.reshape(b, l, SSD_GROUPS, SSD_STATE)
    dt = jax.nn.softplus(jnp.stack([dt_f, dt_b], axis=0).astype(f32)
                         + dt_bias.astype(f32)[:, None, None, :])
    dt = dt.reshape(2, b, l, SSD_GROUPS, SSD_HPG)
    a_neg = -jnp.exp(a_log.astype(f32)).reshape(2, SSD_GROUPS, SSD_HPG)
    y_f, h_f = _ssd_chunked(xh, dt[0], a_neg[0], bm, cm, h0_f)
    y_r, h_b = _ssd_chunked(_flip_seq(xh), _flip_seq(dt[1]), a_neg[1],
                            _flip_seq(bm), _flip_seq(cm), h0_b)
    y = (y_f + _flip_seq(y_r)
         + xh.astype(f32) * d_skip.astype(f32).reshape(SSD_GROUPS, SSD_HPG)[:, :, None])
    y = y.reshape(b, l, SSD_INNER).astype(z.dtype)
    y = _group_rmsnorm(y * jax.nn.silu(z), norm_w, SSD_GROUPS)
    return y, h_f, h_b


def _chunk_gmlp(u, v, norm_w, ws, bs):
    b, l, _ = u.shape
    u = jax.nn.gelu(u)
    v = _rmsnorm(jax.nn.gelu(v), norm_w)
    vc = v.reshape(b, l // CHUNK, CHUNK, GMLP_GROUPS, GMLP_GROUP_DIM)
    mixed = (jnp.einsum('gts,bcsgd->bctgd', ws, vc)
             + jnp.swapaxes(bs, 0, 1)[:, :, None])
    return u * mixed.reshape(b, l, GMLP_WIDTH)


def _moe(h, router_w, router_bias, w_gate, w_up, w_down):
    f32 = jnp.float32
    shp = h.shape
    t = h.reshape(-1, shp[-1])
    n = t.shape[0]
    scores = jax.nn.sigmoid((t @ router_w).astype(f32))
    sel = scores + router_bias.astype(f32)
    group_score = lax.top_k(sel.reshape(n, N_EXPERT_GROUPS, EXPERTS_PER_GROUP), TOP_K)[0].sum(-1)
    best = jnp.argmax(group_score, axis=-1)
    in_group = (jnp.arange(N_EXPERTS) // EXPERTS_PER_GROUP)[None, :] == best[:, None]
    _, idx = lax.top_k(jnp.where(in_group, sel, -jnp.inf), TOP_K)
    w = jnp.take_along_axis(scores, idx, axis=-1)
    w = w / jnp.sum(w, axis=-1, keepdims=True)
    gates = jnp.einsum('tk,tke->et', w, jax.nn.one_hot(idx, N_EXPERTS, dtype=f32)).astype(t.dtype)

    def expert(acc, inp):
        wg, wu, wd, g = inp
        y = (jax.nn.silu(t @ wg) * (t @ wu)) @ wd
        return acc + g[:, None] * y, None

    out, _ = lax.scan(expert, jnp.zeros_like(t), (w_gate, w_up, w_down, gates))
    return out.reshape(shp)


def _layer(x, mod, p, router_w, router_bias, h0_f, h0_b):
    sh1, sc1, g1, sh2, sc2, g2 = jnp.split(mod, N_MOD, axis=-1)
    h = _rmsnorm(x, p['norm1_w']) * (1 + sc1) + sh1
    proj = h @ p['w_in']
    gate_a, gate_b, u, v, z, xbc, dt_f, dt_b = jnp.split(proj, IN_SPLITS, axis=-1)
    y_a = _chunk_gmlp(u, v, p['gmlp_norm_w'], p['gmlp_ws'], p['gmlp_bs'])
    y_b, h_f, h_b = _bi_ssd(xbc, z, dt_f, dt_b, p['conv_w'], p['conv_b'], p['dt_bias'],
                            p['a_log'], p['d_skip'], p['ssd_norm_w'], h0_f, h0_b)
    merged = (jax.nn.sigmoid(gate_a) * (y_a @ p['w_branch_a'])
              + jax.nn.sigmoid(gate_b) * (y_b @ p['w_branch_b']))
    x = x + g1 * (merged @ p['w_out'])
    h2 = _rmsnorm(x, p['norm2_w']) * (1 + sc2) + sh2
    x = x + g2 * _moe(h2, router_w, router_bias, p['w_gate_e'], p['w_up_e'], p['w_down_e'])
    return x, h_f, h_b


def setup_inputs(seed: int = 0) -> dict:
    key = jax.random.key(seed)
    ks = jax.random.split(key, 32)
    nrm = jax.random.normal
    f32 = jnp.float32
    D = D_MODEL
    dt0 = jnp.exp(jax.random.uniform(ks[14], (DEPTH, 2, SSD_HEADS), f32,
                                     minval=math.log(1e-3), maxval=math.log(1e-1)))
    return {
        'x_prompt': nrm(ks[0], (BATCH, SEQ, D), f32),
        'x_sample': nrm(ks[1], (DEC_BATCH, DEC_SEQ, D), f32),
        'state_ssd': 0.1 * nrm(ks[2], (DEC_BATCH, DEPTH, 2, SSD_HEADS, SSD_HEADDIM, SSD_STATE), f32),
        'c': nrm(ks[3], (DEC_BATCH, D), f32),
        'c_ctx': nrm(ks[4], (D,), f32),
        'w_mod': 0.5 * D ** -0.5 * nrm(ks[5], (DEPTH, D, N_MOD * D), f32),
        'b_mod': 0.02 * nrm(ks[6], (DEPTH, N_MOD * D), f32),
        'norm1_w': 1.0 + 0.02 * nrm(ks[7], (DEPTH, D), f32),
        'w_in': D ** -0.5 * nrm(ks[8], (DEPTH, D, IN_PROJ_DIM), f32),
        'gmlp_norm_w': 1.0 + 0.02 * nrm(ks[9], (DEPTH, GMLP_WIDTH), f32),
        'gmlp_ws': CHUNK ** -0.5 * nrm(ks[10], (DEPTH, GMLP_GROUPS, CHUNK, CHUNK), f32),
        'gmlp_bs': 0.02 * nrm(ks[11], (DEPTH, GMLP_GROUPS, CHUNK), f32),
        'conv_w': SSD_CONV ** -0.5 * nrm(ks[12], (DEPTH, SSD_CONV, SSD_CONV_CH), f32),
        'conv_b': 0.02 * nrm(ks[13], (DEPTH, SSD_CONV_CH), f32),
        'dt_bias': dt0 + jnp.log(-jnp.expm1(-dt0)),
        'a_log': jnp.log(jax.random.uniform(ks[15], (DEPTH, 2, SSD_HEADS), f32, minval=1.0, maxval=16.0)),
        'd_skip': 1.0 + 0.02 * nrm(ks[16], (DEPTH, SSD_HEADS), f32),
        'ssd_norm_w': 1.0 + 0.02 * nrm(ks[17], (DEPTH, SSD_INNER), f32),
        'w_branch_a': GMLP_WIDTH ** -0.5 * nrm(ks[18], (DEPTH, GMLP_WIDTH, D), f32),
        'w_branch_b': SSD_INNER ** -0.5 * nrm(ks[19], (DEPTH, SSD_INNER, D), f32),
        'w_out': D ** -0.5 * nrm(ks[20], (DEPTH, D, D), f32),
        'norm2_w': 1.0 + 0.02 * nrm(ks[21], (DEPTH, D), f32),
        'router_w': D ** -0.5 * nrm(ks[22], (D, N_EXPERTS), f32),
        'router_bias': 0.01 * nrm(ks[23], (N_EXPERTS,), f32),
        'w_gate_e': D ** -0.5 * nrm(ks[24], (DEPTH, N_EXPERTS, D, D_EXPERT), f32),
        'w_up_e': D ** -0.5 * nrm(ks[25], (DEPTH, N_EXPERTS, D, D_EXPERT), f32),
        'w_down_e': D_EXPERT ** -0.5 * nrm(ks[26], (DEPTH, N_EXPERTS, D_EXPERT, D), f32),
        'final_norm_w': 1.0 + 0.02 * nrm(ks[27], (D,), f32),
    }


def reference(x_prompt, x_sample, state_ssd, c, c_ctx, w_mod, b_mod, norm1_w, w_in,
              gmlp_norm_w, gmlp_ws, gmlp_bs, conv_w, conv_b, dt_bias, a_log, d_skip,
              ssd_norm_w, w_branch_a, w_branch_b, w_out, norm2_w, router_w, router_bias,
              w_gate_e, w_up_e, w_down_e, final_norm_w):
    bp = x_prompt.shape[0]
    bs, ls = x_sample.shape[0], x_sample.shape[1]
    xp = x_prompt
    xs = x_sample + _grid_pos_embed(ls, D_MODEL, x_sample.dtype)[None]
    zero_state = jnp.zeros((bp, SSD_GROUPS, SSD_HPG, SSD_HEADDIM, SSD_STATE), jnp.float32)
    cache = state_ssd.reshape(bs, DEPTH, 2, SSD_GROUPS, SSD_HPG, SSD_HEADDIM, SSD_STATE)
    new_states = []
    for l in range(DEPTH):
        p = {
            'norm1_w': norm1_w[l], 'w_in': w_in[l], 'gmlp_norm_w': gmlp_norm_w[l],
            'gmlp_ws': gmlp_ws[l], 'gmlp_bs': gmlp_bs[l], 'conv_w': conv_w[l],
            'conv_b': conv_b[l], 'dt_bias': dt_bias[l], 'a_log': a_log[l],
            'd_skip': d_skip[l], 'ssd_norm_w': ssd_norm_w[l], 'w_branch_a': w_branch_a[l],
            'w_branch_b': w_branch_b[l], 'w_out': w_out[l], 'norm2_w': norm2_w[l],
            'w_gate_e': w_gate_e[l], 'w_up_e': w_up_e[l], 'w_down_e': w_down_e[l],
        }
        mod_ctx = _modulation(c_ctx[None, None, :], w_mod[l], b_mod[l])
        mod_lat = _modulation(c[:, None, :], w_mod[l], b_mod[l])
        xp, h_f, h_b = _layer(xp, mod_ctx, p, router_w, router_bias, zero_state, zero_state)
        new_states.append(jnp.stack([h_f, h_b], axis=1))
        xs, _, _ = _layer(xs, mod_lat, p, router_w, router_bias, cache[:, l, 0], cache[:, l, 1])
    new_state_ssd = jnp.stack(new_states, axis=1).reshape(
        bp, DEPTH, 2, SSD_HEADS, SSD_HEADDIM, SSD_STATE).astype(x_prompt.dtype)
    y_prompt = _rmsnorm(xp, final_norm_w)
    y_sample = _rmsnorm(xs, final_norm_w)
    return (y_prompt, y_sample, new_state_ssd)
```

```python
import functools
import math

import numpy as np
import jax
import jax.numpy as jnp
from jax import lax
from jax.experimental import pallas as pl
from jax.experimental.pallas import tpu as pltpu

F32 = jnp.float32
BF16 = jnp.bfloat16

D_MODEL = 1024
CHUNK = 128
GRID_W = 64
GMLP_WIDTH = 1024
GMLP_GROUPS = 8
SSD_INNER = 2048
SSD_HEADDIM = 64
SSD_HEADS = 32
SSD_GROUPS = 4
SSD_HPG = 8
SSD_STATE = 128
SSD_CONV = 5
GROUP_CH = SSD_INNER // SSD_GROUPS
N_EXPERTS = 16
N_EXPERT_GROUPS = 4
EXPERTS_PER_GROUP = 4
D_EXPERT = 512
N_MOD = 6
EPS = 1e-6
MAIN_COLS = 2 * D_MODEL + 2 * GMLP_WIDTH + SSD_INNER + SSD_INNER + 2 * SSD_GROUPS * SSD_STATE
COL_Z = 2 * D_MODEL + 2 * GMLP_WIDTH
COL_X = COL_Z + SSD_INNER
COL_B = COL_X + SSD_INNER
COL_C = COL_B + SSD_GROUPS * SSD_STATE
DT_PAD = 128
MOD_ROWS = 16

V7X_VMEM_LIMIT = 56 * 1024 * 1024


def _cparams(sem):
    return pltpu.CompilerParams(dimension_semantics=sem, vmem_limit_bytes=V7X_VMEM_LIMIT)


def _pick_tile(rows, preferred):
    tm = preferred
    while rows % tm:
        tm //= 2
    assert tm >= CHUNK
    return tm


def _split3(a):
    hi = a.astype(BF16)
    r1 = a - hi.astype(F32)
    mid = r1.astype(BF16)
    lo = (r1 - mid.astype(F32)).astype(BF16)
    return hi, mid, lo


def _split2(a):
    hi = a.astype(BF16)
    lo = (a - hi.astype(F32)).astype(BF16)
    return hi, lo


def _dot(a, b):
    return jnp.dot(a, b, preferred_element_type=F32)


def _dot_nt(a, b):
    return lax.dot_general(a, b, (((1,), (1,)), ((), ())), preferred_element_type=F32)


def _silu(x):
    return x * jax.nn.sigmoid(x)


def _gelu_tanh(x):
    c = math.sqrt(2.0 / math.pi)
    return x * (0.5 * (1.0 + jnp.tanh(c * (x + 0.044715 * (x * x * x)))))


def _softplus(x):
    return jnp.maximum(x, 0.0) + jnp.log1p(jnp.exp(-jnp.abs(x)))


def _rms(x):
    return x * lax.rsqrt(jnp.mean(x * x, axis=-1, keepdims=True) + EPS)


def _mod_kernel(cond_ref, w_ref, b_ref, o_ref):
    c = cond_ref[...]
    s_hi, s_lo = _split2(_silu(c))
    w_hi, w_lo = _split2(w_ref[0])
    o_ref[0] = _dot(s_hi, w_hi) + _dot(s_hi, w_lo) + _dot(s_lo, w_hi) + b_ref[0]


def _modulation(cond, w_mod, b_mod):
    depth, d, n = w_mod.shape
    tn = 1536
    return pl.pallas_call(
        _mod_kernel,
        grid=(depth, n // tn),
        in_specs=[pl.BlockSpec((MOD_ROWS, d), lambda l, j: (0, 0)),
                  pl.BlockSpec((1, d, tn), lambda l, j: (l, 0, j)),
                  pl.BlockSpec((1, 1, tn), lambda l, j: (l, 0, j))],
        out_specs=pl.BlockSpec((1, MOD_ROWS, tn), lambda l, j: (l, 0, j)),
        out_shape=jax.ShapeDtypeStruct((depth, MOD_ROWS, n), F32),
        compiler_params=_cparams(("arbitrary", "arbitrary")),
        name="modulation",
    )(cond, w_mod, b_mod.reshape(depth, 1, n))


def _inproj_kernel(*refs, has_pos, tm):
    if has_pos:
        x_ref, pos_ref, mod_ref, nw_ref, w_ref, wdt_ref, proj_ref, dtc_ref, dtr_ref, h_scr = refs
    else:
        x_ref, mod_ref, nw_ref, w_ref, wdt_ref, proj_ref, dtc_ref, dtr_ref, h_scr = refs
        pos_ref = None

    @pl.when(pl.program_id(1) == 0)
    def _():
        x = x_ref[...]
        if has_pos:
            x = x + pos_ref[...]
        h = (_rms(x) * nw_ref[...]) * (1.0 + mod_ref[0, 1:2, :]) + mod_ref[0, 0:1, :]
        hb = h.astype(BF16)
        h_scr[...] = hb
        dt = _dot(hb, wdt_ref[...])
        for g in range(SSD_GROUPS):
            dtc_ref[g] = dt[:, g * 16:(g + 1) * 16]
        dtt = dt.T
        for g in range(SSD_GROUPS):
            dtr_ref[g] = dtt[g * 16:(g + 1) * 16, :]

    proj_ref[...] = _dot(h_scr[...], w_ref[...]).astype(BF16)


def _inproj(x, pos, mod, norm_w, w_main, w_dt, *, seq_len):
    t, d = x.shape
    tn = 1536
    nb = mod.shape[0]
    rows_per_mod = t // nb
    tm = _pick_tile(rows_per_mod, 1024)
    has_pos = pos is not None
    in_specs = [pl.BlockSpec((tm, d), lambda i, j: (i, 0))]
    args = [x]
    if has_pos:
        pos_blocks = pos.shape[0] // tm
        in_specs.append(pl.BlockSpec((tm, d), lambda i, j: (i % pos_blocks, 0)))
        args.append(pos)
    in_specs += [pl.BlockSpec((1, N_MOD, d), lambda i, j: ((i * tm) // rows_per_mod, 0, 0)),
                 pl.BlockSpec((1, d), lambda i, j: (0, 0)),
                 pl.BlockSpec((d, tn), lambda i, j: (0, j)),
                 pl.BlockSpec((d, DT_PAD), lambda i, j: (0, 0))]
    args += [mod, norm_w, w_main, w_dt]
    return pl.pallas_call(
        functools.partial(_inproj_kernel, has_pos=has_pos, tm=tm),
        grid=(t // tm, MAIN_COLS // tn),
        in_specs=in_specs,
        out_specs=[pl.BlockSpec((tm, tn), lambda i, j: (i, j)),
                   pl.BlockSpec((SSD_GROUPS, tm, 16), lambda i, j: (0, i, 0)),
                   pl.BlockSpec((SSD_GROUPS, 16, tm), lambda i, j: (0, 0, i))],
        out_shape=[jax.ShapeDtypeStruct((t, MAIN_COLS), BF16),
                   jax.ShapeDtypeStruct((SSD_GROUPS, t, 16), F32),
                   jax.ShapeDtypeStruct((SSD_GROUPS, 16, t), F32)],
        scratch_shapes=[pltpu.VMEM((tm, d), BF16)],
        compiler_params=_cparams(("arbitrary", "arbitrary")),
        name="inproj",
    )(*args)


def _tri_dot_left(tri, a):
    hi, mid, lo = _split3(a)
    return _dot(tri, hi) + _dot(tri, mid) + _dot(tri, lo)


def _tri_dot_right(a, tri):
    hi, mid, lo = _split3(a)
    return _dot(hi, tri) + _dot(mid, tri) + _dot(lo, tri)


def _ssd_kernel(*refs, seq_len, has_h0, emit_state):
    it = iter(refs)
    x_ref, b_ref, c_ref, z_ref, dtc_ref, dtr_ref = (next(it) for _ in range(6))
    cwx_ref, cwb_ref, cwc_ref, cbx_ref, cbb_ref, cbc_ref = (next(it) for _ in range(6))
    prow_ref, pcol_ref, dsk_ref, nw_ref, expf_ref, expb_ref = (next(it) for _ in range(6))
    h0_ref = next(it) if has_h0 else None
    y_ref = next(it)
    st_ref = next(it) if emit_state else None
    pad_scr, cv_scr, yb_scr, dtc_scr, dtr_scr, hf_scr, hb_scr, m_scr, yd_scr = (next(it) for _ in range(9))

    L = seq_len
    nc = L // CHUNK
    W = GROUP_CH + 2 * SSD_STATE

    zeros8 = jnp.zeros((8, W), F32)
    pad_scr[0:8, :] = zeros8
    pad_scr[L + 8:L + 16, :] = zeros8

    def stage(c, carry):
        r0 = pl.multiple_of(c * CHUNK, CHUNK)
        pad_scr[pl.ds(r0 + 8, CHUNK), 0:GROUP_CH] = x_ref[0, pl.ds(r0, CHUNK), :].astype(F32)
        pad_scr[pl.ds(r0 + 8, CHUNK), GROUP_CH:GROUP_CH + SSD_STATE] = b_ref[0, pl.ds(r0, CHUNK), :].astype(F32)
        pad_scr[pl.ds(r0 + 8, CHUNK), GROUP_CH + SSD_STATE:W] = c_ref[0, pl.ds(r0, CHUNK), :].astype(F32)
        return carry

    lax.fori_loop(0, nc, stage, 0)

    cw = [cwx_ref[...][:, k * 128:(k + 1) * 128] for k in range(GROUP_CH // 128)] + [cwb_ref[...], cwc_ref[...]]
    cb = [cbx_ref[...][:, k * 128:(k + 1) * 128] for k in range(GROUP_CH // 128)] + [cbb_ref[...], cbc_ref[...]]
    win_rows = CHUNK + 16

    def conv(c, carry):
        r0 = pl.multiple_of(c * CHUNK, CHUNK)
        for k in range(W // 128):
            win = pad_scr[pl.ds(r0, win_rows), k * 128:(k + 1) * 128]
            acc = win[8:8 + CHUNK, :] * cw[k][2:3, :]
            for tap in (0, 1, 3, 4):
                d = tap - 2
                rolled = pltpu.roll(win, (win_rows - d) % win_rows, 0)
                acc = acc + rolled[8:8 + CHUNK, :] * cw[k][tap:tap + 1, :]
            acc = acc + cb[k]
            cv_scr[pl.ds(r0, CHUNK), k * 128:(k + 1) * 128] = _silu(acc)
        return carry

    lax.fori_loop(0, nc, conv, 0)

    bias_row = prow_ref[0, 0:1, :]
    aneg_row = -jnp.exp(prow_ref[0, 1:2, :])
    bias_col = pcol_ref[0, :, 0:1]
    aneg_col = -jnp.exp(pcol_ref[0, :, 1:2])
    dtc_scr[...] = _softplus(dtc_ref[0] + bias_row)
    for c in range(nc):
        dtr_scr[c] = _softplus(dtr_ref[0, :, c * CHUNK:(c + 1) * CHUNK] + bias_col)

    ri = lax.broadcasted_iota(jnp.int32, (CHUNK, CHUNK), 0)
    ci = lax.broadcasted_iota(jnp.int32, (CHUNK, CHUNK), 1)
    lower = ci <= ri
    tri_le = jnp.where(lower, 1.0, 0.0).astype(BF16)
    tri_ge = jnp.where(ci >= ri, 1.0, 0.0).astype(BF16)
    diag = ci == ri
    lane16 = lax.broadcasted_iota(jnp.int32, (CHUNK, 16), 1)
    row16 = lax.broadcasted_iota(jnp.int32, (16, CHUNK), 0)

    def col_scalars(r0):
        dt = dtc_scr[pl.ds(r0, CHUNK), :]
        a = dt * aneg_row
        acs = jnp.where(lane16 < SSD_HPG, _tri_dot_left(tri_le, a), _tri_dot_left(tri_ge, a))
        last = jnp.where(lane16[0:1, :] < SSD_HPG, acs[CHUNK - 1:CHUNK, :], acs[0:1, :])
        e = jnp.exp(acs)
        w = dt * jnp.exp(last - acs)
        return dt, acs, e, w

    def expand(e, w, exp_ref):
        e_hi, e_lo = _split2(e)
        w_hi, w_lo = _split2(w)
        m_scr[:, 0:16] = e_hi.astype(F32)
        m_scr[:, 16:32] = e_lo.astype(F32)
        m_scr[:, 32:48] = w_hi.astype(F32)
        m_scr[:, 48:64] = w_lo.astype(F32)
        full = _dot(m_scr[...].astype(BF16), exp_ref[...])
        return full[:, 0:GROUP_CH], full[:, GROUP_CH:2 * GROUP_CH]

    if has_h0:
        hf_scr[...] = h0_ref[0, 0, 0, 0].T
        hb_scr[...] = h0_ref[0, 0, 1, 0].T
    else:
        hf_scr[...] = jnp.zeros((SSD_STATE, GROUP_CH), F32)
        hb_scr[...] = jnp.zeros((SSD_STATE, GROUP_CH), F32)

    def bwd(i, carry):
        c = nc - 1 - i
        r0 = pl.multiple_of(c * CHUNK, CHUNK)
        xs = cv_scr[pl.ds(r0, CHUNK), 0:GROUP_CH]
        bm = cv_scr[pl.ds(r0, CHUNK), GROUP_CH:GROUP_CH + SSD_STATE]
        cm = cv_scr[pl.ds(r0, CHUNK), GROUP_CH + SSD_STATE:W]
        _, _, e, w = col_scalars(r0)
        e_x, w_x = expand(e, w, expb_ref)
        h = hb_scr[...]
        yb_scr[pl.ds(r0, CHUNK), :] = _dot(cm.astype(BF16), h.astype(BF16)) * e_x
        hb_scr[...] = h * e_x[0:1, :] + _dot(bm.T.astype(BF16), (xs * w_x).astype(BF16))
        return carry

    lax.fori_loop(0, nc, bwd, 0)

    dsk = dsk_ref[...]
    nw = nw_ref[...]

    def fwd(c, carry):
        r0 = pl.multiple_of(c * CHUNK, CHUNK)
        xs = cv_scr[pl.ds(r0, CHUNK), 0:GROUP_CH]
        bm = cv_scr[pl.ds(r0, CHUNK), GROUP_CH:GROUP_CH + SSD_STATE]
        cm = cv_scr[pl.ds(r0, CHUNK), GROUP_CH + SSD_STATE:W]
        dt, acs, e, w = col_scalars(r0)
        e_x, w_x = expand(e, w, expf_ref)

        dtr = dtr_scr[c]
        a_r = dtr * aneg_col
        acs_r = jnp.where(row16 < SSD_HPG, _tri_dot_right(a_r, tri_ge), _tri_dot_right(a_r, tri_le))
        q_r = acs_r - jnp.log(dtr)

        cmb = cm.astype(BF16)
        cb_mat = _dot_nt(cmb, bm.astype(BF16))
        for hh in range(SSD_HPG):
            pf = acs[:, hh:hh + 1]
            pb = acs[:, SSD_HPG + hh:SSD_HPG + hh + 1]
            qf = q_r[hh:hh + 1, :]
            qb = q_r[SSD_HPG + hh:SSD_HPG + hh + 1, :]
            dtb = dt[:, SSD_HPG + hh:SSD_HPG + hh + 1]
            arg = jnp.where(lower, pf - qf, pb - qb)
            wmat = cb_mat * (jnp.exp(arg) + jnp.where(diag, dtb, 0.0))
            yd_scr[:, hh * SSD_HEADDIM:(hh + 1) * SSD_HEADDIM] = _dot(
                wmat.astype(BF16), xs[:, hh * SSD_HEADDIM:(hh + 1) * SSD_HEADDIM].astype(BF16))

        h = hf_scr[...]
        y = yb_scr[pl.ds(r0, CHUNK), :] + _dot(cmb, h.astype(BF16)) * e_x + yd_scr[...] + xs * dsk
        hf_scr[...] = h * e_x[CHUNK - 1:CHUNK, :] + _dot(bm.T.astype(BF16), (xs * w_x).astype(BF16))

        zz = z_ref[0, pl.ds(r0, CHUNK), :].astype(F32)
        y_ref[0, pl.ds(r0, CHUNK), :] = (_rms(y * _silu(zz)) * nw).astype(BF16)
        return carry

    lax.fori_loop(0, nc, fwd, 0)

    if emit_state:
        st_ref[0, 0, 0] = hf_scr[...].T
        st_ref[0, 1, 0] = hb_scr[...].T


def _expander(first_row):
    m = np.zeros((64, 2 * GROUP_CH), np.float32)
    for blk in range(4):
        half = blk // 2
        for j in range(SSD_HPG):
            m[blk * 16 + first_row + j, half * GROUP_CH + j * SSD_HEADDIM: half * GROUP_CH + (j + 1) * SSD_HEADDIM] = 1.0
    return jnp.asarray(m, BF16)


def _ssd(proj, dtc, dtr, lw, h0, *, batch, seq_len, layer, emit_state):
    L = seq_len
    proj3 = proj.reshape(batch, L, MAIN_COLS)
    has_h0 = h0 is not None
    gc = GROUP_CH
    W = gc + 2 * SSD_STATE

    in_specs = [
        pl.BlockSpec((1, L, gc), lambda b, g: (b, 0, COL_X // gc + g)),
        pl.BlockSpec((1, L, SSD_STATE), lambda b, g: (b, 0, COL_B // SSD_STATE + g)),
        pl.BlockSpec((1, L, SSD_STATE), lambda b, g: (b, 0, COL_C // SSD_STATE + g)),
        pl.BlockSpec((1, L, gc), lambda b, g: (b, 0, COL_Z // gc + g)),
        pl.BlockSpec((1, L, 16), lambda b, g: (g, b, 0)),
        pl.BlockSpec((1, 16, L), lambda b, g: (g, 0, b)),
        pl.BlockSpec((SSD_CONV, gc), lambda b, g: (0, g)),
        pl.BlockSpec((SSD_CONV, SSD_STATE), lambda b, g: (0, SSD_INNER // SSD_STATE + g)),
        pl.BlockSpec((SSD_CONV, SSD_STATE), lambda b, g: (0, SSD_INNER // SSD_STATE + SSD_GROUPS + g)),
        pl.BlockSpec((1, gc), lambda b, g: (0, g)),
        pl.BlockSpec((1, SSD_STATE), lambda b, g: (0, SSD_INNER // SSD_STATE + g)),
        pl.BlockSpec((1, SSD_STATE), lambda b, g: (0, SSD_INNER // SSD_STATE + SSD_GROUPS + g)),
        pl.BlockSpec((1, 2, 16), lambda b, g: (g, 0, 0)),
        pl.BlockSpec((1, 16, 2), lambda b, g: (g, 0, 0)),
        pl.BlockSpec((1, gc), lambda b, g: (0, g)),
        pl.BlockSpec((1, gc), lambda b, g: (0, g)),
        pl.BlockSpec((64, 2 * gc), lambda b, g: (0, 0)),
        pl.BlockSpec((64, 2 * gc), lambda b, g: (0, 0)),
    ]
    args = [proj3, proj3, proj3, proj3, dtc, dtr,
            lw["conv_w"], lw["conv_w"], lw["conv_w"], lw["conv_b"], lw["conv_b"], lw["conv_b"],
            lw["ssd_prow"], lw["ssd_pcol"], lw["d_skip_x"], lw["ssd_norm_w"],
            _expander(0), _expander(SSD_HPG)]
    if has_h0:
        in_specs.append(pl.BlockSpec((1, 1, 2, 1, gc, SSD_STATE), lambda b, g: (b, layer, 0, g, 0, 0)))
        args.append(h0)
    out_specs = [pl.BlockSpec((1, L, gc), lambda b, g: (b, 0, g))]
    out_shape = [jax.ShapeDtypeStruct((batch, L, SSD_INNER), BF16)]
    if emit_state:
        out_specs.append(pl.BlockSpec((1, 2, 1, gc, SSD_STATE), lambda b, g: (b, 0, g, 0, 0)))
        out_shape.append(jax.ShapeDtypeStruct((batch, 2, SSD_GROUPS, gc, SSD_STATE), F32))
    nc = L // CHUNK
    scratch = [
        pltpu.VMEM((L + 16, W), F32),
        pltpu.VMEM((L, W), F32),
        pltpu.VMEM((L, gc), F32),
        pltpu.VMEM((L, 16), F32),
        pltpu.VMEM((nc, 16, CHUNK), F32),
        pltpu.VMEM((SSD_STATE, gc), F32),
        pltpu.VMEM((SSD_STATE, gc), F32),
        pltpu.VMEM((CHUNK, 64), F32),
        pltpu.VMEM((CHUNK, gc), F32),
    ]
    outs = pl.pallas_call(
        functools.partial(_ssd_kernel, seq_len=L, has_h0=has_h0, emit_state=emit_state),
        grid=(batch, SSD_GROUPS),
        in_specs=in_specs,
        out_specs=out_specs,
        out_shape=out_shape,
        scratch_shapes=scratch,
        compiler_params=_cparams(("arbitrary", "arbitrary")),
        name="ssd",
    )(*args)
    y = outs[0].reshape(batch * L, SSD_INNER)
    return y, (outs[1] if emit_state else None)


def _route(sel, scores):
    neg = -jnp.inf

    def first_max(vals):
        m = vals[0]
        for v in vals[1:]:
            m = jnp.maximum(m, v)
        taken = jnp.zeros_like(m)
        flags = []
        for v in vals:
            f = jnp.where(v == m, 1.0, 0.0) * (1.0 - taken)
            flags.append(f)
            taken = taken + f
        return m, flags

    group_scores = []
    for j in range(N_EXPERT_GROUPS):
        a = sel[j * EXPERTS_PER_GROUP:(j + 1) * EXPERTS_PER_GROUP]
        m1, f1 = first_max(a)
        m2, _ = first_max([jnp.where(f > 0.5, neg, v) for f, v in zip(f1, a)])
        group_scores.append(m1 + m2)
    _, gflag = first_max(group_scores)
    masked = [jnp.where(gflag[e // EXPERTS_PER_GROUP] > 0.5, sel[e], neg) for e in range(N_EXPERTS)]
    _, f1 = first_max(masked)
    _, f2 = first_max([jnp.where(f > 0.5, neg, v) for f, v in zip(f1, masked)])
    w1 = sum(f * s for f, s in zip(f1, scores))
    w2 = sum(f * s for f, s in zip(f2, scores))
    tot = w1 + w2
    return [(f1[e] * w1 + f2[e] * w2) / tot for e in range(N_EXPERTS)]


def _mixout_kernel(*refs, has_pos, tm):
    it = iter(refs)
    x_ref = next(it)
    pos_ref = next(it) if has_pos else None
    ga_ref, gb_ref, u_ref, v_ref, yb_ref, mod_ref = (next(it) for _ in range(6))
    gnw_ref, ws_ref, bsx_ref, wa_ref, wb_ref, wo_ref, n2w_ref, rwt_ref, rb_ref = (next(it) for _ in range(9))
    xo_ref, h2_ref, gt_ref = (next(it) for _ in range(3))
    g_scr = next(it)

    x = x_ref[...]
    if has_pos:
        x = x + pos_ref[...]

    u = _gelu_tanh(u_ref[...].astype(F32))
    vn = (_rms(_gelu_tanh(v_ref[...].astype(F32))) * gnw_ref[...]).astype(BF16)
    nch = tm // CHUNK
    gd = GMLP_WIDTH // GMLP_GROUPS
    mixed_cols = [[None] * GMLP_GROUPS for _ in range(nch)]
    for g in range(GMLP_GROUPS):
        rhs = jnp.concatenate([vn[c * CHUNK:(c + 1) * CHUNK, g * gd:(g + 1) * gd] for c in range(nch)], axis=1)
        res = _dot(ws_ref[g], rhs)
        for c in range(nch):
            mixed_cols[c][g] = res[:, c * gd:(c + 1) * gd]
    mixed = jnp.concatenate([jnp.concatenate(mixed_cols[c], axis=1) + bsx_ref[...] for c in range(nch)], axis=0)
    ya = (u * mixed).astype(BF16)

    merged = (jax.nn.sigmoid(ga_ref[...].astype(F32)) * _dot(ya, wa_ref[...])
              + jax.nn.sigmoid(gb_ref[...].astype(F32)) * _dot(yb_ref[...], wb_ref[...]))
    xn = x + mod_ref[0, 2:3, :] * _dot(merged.astype(BF16), wo_ref[...])
    xo_ref[...] = xn

    h2 = (_rms(xn) * n2w_ref[...]) * (1.0 + mod_ref[0, 4:5, :]) + mod_ref[0, 3:4, :]
    h2_ref[...] = h2.astype(BF16)

    h_hi, h_lo = _split2(h2)
    r_hi, r_lo = _split2(rwt_ref[...])
    logits = _dot_nt(r_hi, h_hi) + _dot_nt(r_hi, h_lo) + _dot_nt(r_lo, h_hi)
    scores = jax.nn.sigmoid(logits)
    selm = scores + rb_ref[...]
    gates = _route([selm[e:e + 1, :] for e in range(N_EXPERTS)], [scores[e:e + 1, :] for e in range(N_EXPERTS)])
    g_scr[...] = jnp.zeros(g_scr.shape, F32)
    for e in range(N_EXPERTS):
        g_scr[e:e + 1, :] = gates[e]
    gt_ref[...] = g_scr[...].T[:, 0:N_EXPERTS]


def _mixout(x, pos, proj, yb, mod, lw, *, seq_len):
    t, d = x.shape
    tm = 256
    nb = mod.shape[0]
    rows_per_mod = t // nb
    has_pos = pos is not None
    full = lambda shape: pl.BlockSpec(shape, lambda i: (0,) * len(shape))
    in_specs = [pl.BlockSpec((tm, d), lambda i: (i, 0))]
    args = [x]
    if has_pos:
        pos_blocks = pos.shape[0] // tm
        in_specs.append(pl.BlockSpec((tm, d), lambda i: (i % pos_blocks, 0)))
        args.append(pos)
    in_specs += [pl.BlockSpec((tm, d), lambda i: (i, 0)),
                 pl.BlockSpec((tm, d), lambda i: (i, 1)),
                 pl.BlockSpec((tm, d), lambda i: (i, 2)),
                 pl.BlockSpec((tm, d), lambda i: (i, 3)),
                 pl.BlockSpec((tm, SSD_INNER), lambda i: (i, 0)),
                 pl.BlockSpec((1, N_MOD, d), lambda i: ((i * tm) // rows_per_mod, 0, 0)),
                 full((1, GMLP_WIDTH)),
                 full((GMLP_GROUPS, CHUNK, CHUNK)),
                 full((CHUNK, GMLP_WIDTH)),
                 full((GMLP_WIDTH, d)),
                 full((SSD_INNER, d)),
                 full((d, d)),
                 full((1, d)),
                 full((N_EXPERTS, d)),
                 full((N_EXPERTS, 1))]
    args += [proj, proj, proj, proj, yb, mod,
             lw["gmlp_norm_w"], lw["gmlp_ws"], lw["gmlp_bs_x"], lw["w_branch_a"], lw["w_branch_b"], lw["w_out"],
             lw["norm2_w"], lw["router_wt"], lw["router_bias"]]
    return pl.pallas_call(
        functools.partial(_mixout_kernel, has_pos=has_pos, tm=tm),
        grid=(t // tm,),
        in_specs=in_specs,
        out_specs=[pl.BlockSpec((tm, d), lambda i: (i, 0)),
                   pl.BlockSpec((tm, d), lambda i: (i, 0)),
                   pl.BlockSpec((tm, N_EXPERTS), lambda i: (i, 0))],
        out_shape=[jax.ShapeDtypeStruct((t, d), F32),
                   jax.ShapeDtypeStruct((t, d), BF16),
                   jax.ShapeDtypeStruct((t, N_EXPERTS), F32)],
        scratch_shapes=[pltpu.VMEM((CHUNK, tm), F32)],
        compiler_params=_cparams(("arbitrary",)),
        name="mixout",
    )(*args)


def _moe_kernel(x_ref, h_ref, gt_ref, mod_ref, wg_ref, wu_ref, wd_ref, fnw_ref, o_ref, acc_scr, *, final):
    e = pl.program_id(1)

    @pl.when(e == 0)
    def _():
        acc_scr[...] = jnp.zeros(acc_scr.shape, F32)

    h = h_ref[...]
    act = (_silu(_dot(h, wg_ref[0])) * _dot(h, wu_ref[0])).astype(BF16)
    y = _dot(act, wd_ref[0])
    gt = gt_ref[...]
    lane = lax.broadcasted_iota(jnp.int32, gt.shape, 1)
    g = jnp.sum(jnp.where(lane == e, gt, 0.0), axis=1, keepdims=True)
    acc_scr[...] += g * y

    @pl.when(e == N_EXPERTS - 1)
    def _():
        xn = x_ref[...] + mod_ref[0, 5:6, :] * acc_scr[...]
        if final:
            xn = _rms(xn) * fnw_ref[...]
        o_ref[...] = xn


def _moe(x, h2, gt, mod, lw, final_norm_w, *, final):
    t, d = x.shape
    nb = mod.shape[0]
    rows_per_mod = t // nb
    tm = _pick_tile(rows_per_mod, 1024)
    return pl.pallas_call(
        functools.partial(_moe_kernel, final=final),
        grid=(t // tm, N_EXPERTS),
        in_specs=[pl.BlockSpec((tm, d), lambda i, e: (i, 0)),
                  pl.BlockSpec((tm, d), lambda i, e: (i, 0)),
                  pl.BlockSpec((tm, N_EXPERTS), lambda i, e: (i, 0)),
                  pl.BlockSpec((1, N_MOD, d), lambda i, e: ((i * tm) // rows_per_mod, 0, 0)),
                  pl.BlockSpec((1, d, D_EXPERT), lambda i, e: (e, 0, 0)),
                  pl.BlockSpec((1, d, D_EXPERT), lambda i, e: (e, 0, 0)),
                  pl.BlockSpec((1, D_EXPERT, d), lambda i, e: (e, 0, 0)),
                  pl.BlockSpec((1, d), lambda i, e: (0, 0))],
        out_specs=pl.BlockSpec((tm, d), lambda i, e: (i, 0)),
        out_shape=jax.ShapeDtypeStruct((t, d), F32),
        scratch_shapes=[pltpu.VMEM((tm, d), F32)],
        compiler_params=_cparams(("arbitrary", "arbitrary")),
        name="moe",
    )(x, h2, gt, mod, lw["w_gate_e"], lw["w_up_e"], lw["w_down_e"], final_norm_w)


def _grid_pos_embed(n_tokens, dim):
    rows = n_tokens // GRID_W
    quarter = dim // 4
    omega = 1.0 / (10000.0 ** (jnp.arange(quarter, dtype=F32) / quarter))
    r = jnp.arange(rows, dtype=F32)[:, None] * omega
    col = jnp.arange(GRID_W, dtype=F32)[:, None] * omega
    r_emb = jnp.concatenate([jnp.sin(r), jnp.cos(r)], axis=-1)
    c_emb = jnp.concatenate([jnp.sin(col), jnp.cos(col)], axis=-1)
    emb = jnp.concatenate([
        jnp.broadcast_to(r_emb[:, None, :], (rows, GRID_W, dim // 2)),
        jnp.broadcast_to(c_emb[None, :, :], (rows, GRID_W, dim // 2))], axis=-1)
    return emb.reshape(rows * GRID_W, dim)


def _dt_perm():
    return np.array([dr * SSD_HEADS + g * SSD_HPG + j
                     for g in range(SSD_GROUPS) for dr in range(2) for j in range(SSD_HPG)], np.int32)


def _layer_weights(l, w_in, norm1_w, gmlp_norm_w, gmlp_ws, gmlp_bs, conv_w, conv_b, dt_bias, a_log, d_skip,
                   ssd_norm_w, w_branch_a, w_branch_b, w_out, norm2_w, router_w, router_bias,
                   w_gate_e, w_up_e, w_down_e):
    perm = _dt_perm()
    w_dt = jnp.pad(w_in[l][:, MAIN_COLS:][:, perm], ((0, 0), (0, DT_PAD - 2 * SSD_HEADS)))
    prow = jnp.stack([dt_bias[l].reshape(-1)[perm], a_log[l].reshape(-1)[perm]], axis=0)
    prow = prow.reshape(2, SSD_GROUPS, 16).transpose(1, 0, 2)
    return {
        "norm1_w": norm1_w[l][None],
        "w_main": w_in[l][:, :MAIN_COLS].astype(BF16),
        "w_dt": w_dt.astype(BF16),
        "gmlp_norm_w": gmlp_norm_w[l][None],
        "gmlp_ws": gmlp_ws[l].astype(BF16),
        "gmlp_bs_x": jnp.repeat(gmlp_bs[l].T, GMLP_WIDTH // GMLP_GROUPS, axis=1),
        "conv_w": conv_w[l],
        "conv_b": conv_b[l][None],
        "ssd_prow": prow,
        "ssd_pcol": prow.transpose(0, 2, 1),
        "d_skip_x": jnp.repeat(d_skip[l], SSD_HEADDIM)[None],
        "ssd_norm_w": ssd_norm_w[l][None],
        "w_branch_a": w_branch_a[l].astype(BF16),
        "w_branch_b": w_branch_b[l].astype(BF16),
        "w_out": w_out[l].astype(BF16),
        "norm2_w": norm2_w[l][None],
        "router_wt": router_w.T,
        "router_bias": router_bias[:, None],
        "w_gate_e": w_gate_e[l].astype(BF16),
        "w_up_e": w_up_e[l].astype(BF16),
        "w_down_e": w_down_e[l].astype(BF16),
    }


def _layer(x, pos, mod, lw, h0, final_norm_w, *, batch, seq_len, layer, emit_state, final):
    proj, dtc, dtr = _inproj(x, pos, mod, lw["norm1_w"], lw["w_main"], lw["w_dt"], seq_len=seq_len)
    yb, state = _ssd(proj, dtc, dtr, lw, h0, batch=batch, seq_len=seq_len, layer=layer, emit_state=emit_state)
    xn, h2, gt = _mixout(x, pos, proj, yb, mod, lw, seq_len=seq_len)
    out = _moe(xn, h2, gt, mod, lw, final_norm_w, final=final)
    return out, state


def kernel(x_prompt, x_sample, state_ssd, c, c_ctx, w_mod, b_mod, norm1_w, w_in, gmlp_norm_w, gmlp_ws, gmlp_bs, conv_w, conv_b, dt_bias, a_log, d_skip, ssd_norm_w, w_branch_a, w_branch_b, w_out, norm2_w, router_w, router_bias, w_gate_e, w_up_e, w_down_e, final_norm_w):
    bp, lp, d = x_prompt.shape
    bs, ls, _ = x_sample.shape
    depth = w_mod.shape[0]
    assert 1 + bs <= MOD_ROWS and d == D_MODEL
    assert lp % CHUNK == 0 and ls % CHUNK == 0

    cond = jnp.concatenate([c_ctx[None], c, jnp.zeros((MOD_ROWS - 1 - bs, d), F32)], axis=0)
    mod = _modulation(cond, w_mod, b_mod).reshape(depth, MOD_ROWS, N_MOD, d)

    pos = _grid_pos_embed(ls, d)
    xp = x_prompt.reshape(bp * lp, d)
    xs = x_sample.reshape(bs * ls, d)
    cache = state_ssd.reshape(bs, depth, 2, SSD_GROUPS, GROUP_CH, SSD_STATE)
    fnw = final_norm_w[None]

    states = []
    for l in range(depth):
        lw = _layer_weights(l, w_in, norm1_w, gmlp_norm_w, gmlp_ws, gmlp_bs, conv_w, conv_b, dt_bias, a_log,
                            d_skip, ssd_norm_w, w_branch_a, w_branch_b, w_out, norm2_w, router_w, router_bias,
                            w_gate_e, w_up_e, w_down_e)
        final = l == depth - 1
        xp, st = _layer(xp, None, mod[l, 0:1], lw, None, fnw, batch=bp, seq_len=lp, layer=l,
                        emit_state=True, final=final)
        states.append(st)
        xs, _ = _layer(xs, pos if l == 0 else None, mod[l, 1:1 + bs], lw, cache, fnw, batch=bs, seq_len=ls,
                       layer=l, emit_state=False, final=final)

    new_state = jnp.stack(states, axis=1).reshape(bp, depth, 2, SSD_HEADS, SSD_HEADDIM, SSD_STATE)
    return (xp.reshape(bp, lp, d), xs.reshape(bs, ls, d), new_state.astype(x_prompt.dtype))
```

```python
import functools
import math

import numpy as np
import jax
import jax.numpy as jnp
from jax import lax
from jax.experimental import pallas as pl
from jax.experimental.pallas import tpu as pltpu

F32 = jnp.float32
BF16 = jnp.bfloat16

D_MODEL = 1024
CHUNK = 128
GRID_W = 64
GMLP_WIDTH = 1024
GMLP_GROUPS = 8
SSD_INNER = 2048
SSD_HEADDIM = 64
SSD_HEADS = 32
SSD_GROUPS = 4
SSD_HPG = 8
SSD_STATE = 128
SSD_CONV = 5
GROUP_CH = SSD_INNER // SSD_GROUPS
N_EXPERTS = 16
N_EXPERT_GROUPS = 4
EXPERTS_PER_GROUP = 4
D_EXPERT = 512
N_MOD = 6
EPS = 1e-6
MAIN_COLS = 2 * D_MODEL + 2 * GMLP_WIDTH + SSD_INNER + SSD_INNER + 2 * SSD_GROUPS * SSD_STATE
COL_Z = 2 * D_MODEL + 2 * GMLP_WIDTH
COL_X = COL_Z + SSD_INNER
COL_B = COL_X + SSD_INNER
COL_C = COL_B + SSD_GROUPS * SSD_STATE
DT_PAD = 128
ROW_X = 0
ROW_H = D_MODEL
ROW_AUX = 2 * D_MODEL
ROW_W = 2 * D_MODEL + 128
AUX_MODROW = 4
MOE_TILE = 256
MOD_ROWS = 16

V7X_VMEM_LIMIT = 56 * 1024 * 1024


def _cparams(sem):
    return pltpu.CompilerParams(dimension_semantics=sem, vmem_limit_bytes=V7X_VMEM_LIMIT)


def _pick_tile(rows, preferred):
    tm = preferred
    while rows % tm:
        tm //= 2
    assert tm >= CHUNK
    return tm


def _split3(a):
    hi = a.astype(BF16)
    r1 = a - hi.astype(F32)
    mid = r1.astype(BF16)
    lo = (r1 - mid.astype(F32)).astype(BF16)
    return hi, mid, lo


def _split2(a):
    hi = a.astype(BF16)
    lo = (a - hi.astype(F32)).astype(BF16)
    return hi, lo


def _dot(a, b):
    return jnp.dot(a, b, preferred_element_type=F32)


def _dot_nt(a, b):
    return lax.dot_general(a, b, (((1,), (1,)), ((), ())), preferred_element_type=F32)


def _silu(x):
    return x * jax.nn.sigmoid(x)


def _gelu_tanh(x):
    c = math.sqrt(2.0 / math.pi)
    return x * (0.5 * (1.0 + jnp.tanh(c * (x + 0.044715 * (x * x * x)))))


def _softplus(x):
    return jnp.maximum(x, 0.0) + jnp.log1p(jnp.exp(-jnp.abs(x)))


def _rms(x):
    return x * lax.rsqrt(jnp.mean(x * x, axis=-1, keepdims=True) + EPS)


def _mod_kernel(cond_ref, w_ref, b_ref, o_ref):
    c = cond_ref[...]
    s_hi, s_lo = _split2(_silu(c))
    w_hi, w_lo = _split2(w_ref[0])
    o_ref[0] = _dot(s_hi, w_hi) + _dot(s_hi, w_lo) + _dot(s_lo, w_hi) + b_ref[0]


def _modulation(cond, w_mod, b_mod):
    depth, d, n = w_mod.shape
    tn = 1536
    return pl.pallas_call(
        _mod_kernel,
        grid=(depth, n // tn),
        in_specs=[pl.BlockSpec((MOD_ROWS, d), lambda l, j: (0, 0)),
                  pl.BlockSpec((1, d, tn), lambda l, j: (l, 0, j)),
                  pl.BlockSpec((1, 1, tn), lambda l, j: (l, 0, j))],
        out_specs=pl.BlockSpec((1, MOD_ROWS, tn), lambda l, j: (l, 0, j)),
        out_shape=jax.ShapeDtypeStruct((depth, MOD_ROWS, n), F32),
        compiler_params=_cparams(("arbitrary", "arbitrary")),
        name="modulation",
    )(cond, w_mod, b_mod.reshape(depth, 1, n))


def _inproj_kernel(*refs, has_pos, tm):
    if has_pos:
        x_ref, pos_ref, mod_ref, nw_ref, w_ref, wdt_ref, proj_ref, dtc_ref, dtr_ref, h_scr = refs
    else:
        x_ref, mod_ref, nw_ref, w_ref, wdt_ref, proj_ref, dtc_ref, dtr_ref, h_scr = refs
        pos_ref = None

    @pl.when(pl.program_id(1) == 0)
    def _():
        x = x_ref[...]
        if has_pos:
            x = x + pos_ref[...]
        h = (_rms(x) * nw_ref[...]) * (1.0 + mod_ref[0, 1:2, :]) + mod_ref[0, 0:1, :]
        hb = h.astype(BF16)
        h_scr[...] = hb
        dt = _dot(hb, wdt_ref[...])
        for g in range(SSD_GROUPS):
            dtc_ref[g] = dt[:, g * 16:(g + 1) * 16]
        dtt = dt.T
        for g in range(SSD_GROUPS):
            dtr_ref[g] = dtt[g * 16:(g + 1) * 16, :]

    proj_ref[...] = _dot(h_scr[...], w_ref[...]).astype(BF16)


def _inproj(x, pos, mod, norm_w, w_main, w_dt, *, n_tokens):
    t, d = n_tokens, x.shape[1]
    tn = 1536
    nb = mod.shape[0]
    rows_per_mod = t // nb
    tm = _pick_tile(rows_per_mod, 1024)
    has_pos = pos is not None
    in_specs = [pl.BlockSpec((tm, d), lambda i, j: (i, 0))]
    args = [x]
    if has_pos:
        pos_blocks = pos.shape[0] // tm
        in_specs.append(pl.BlockSpec((tm, d), lambda i, j: (i % pos_blocks, 0)))
        args.append(pos)
    in_specs += [pl.BlockSpec((1, N_MOD, d), lambda i, j: ((i * tm) // rows_per_mod, 0, 0)),
                 pl.BlockSpec((1, d), lambda i, j: (0, 0)),
                 pl.BlockSpec((d, tn), lambda i, j: (0, j)),
                 pl.BlockSpec((d, DT_PAD), lambda i, j: (0, 0))]
    args += [mod, norm_w, w_main, w_dt]
    return pl.pallas_call(
        functools.partial(_inproj_kernel, has_pos=has_pos, tm=tm),
        grid=(t // tm, MAIN_COLS // tn),
        in_specs=in_specs,
        out_specs=[pl.BlockSpec((tm, tn), lambda i, j: (i, j)),
                   pl.BlockSpec((SSD_GROUPS, tm, 16), lambda i, j: (0, i, 0)),
                   pl.BlockSpec((SSD_GROUPS, 16, tm), lambda i, j: (0, 0, i))],
        out_shape=[jax.ShapeDtypeStruct((t, MAIN_COLS), BF16),
                   jax.ShapeDtypeStruct((SSD_GROUPS, t, 16), F32),
                   jax.ShapeDtypeStruct((SSD_GROUPS, 16, t), F32)],
        scratch_shapes=[pltpu.VMEM((tm, d), BF16)],
        compiler_params=_cparams(("arbitrary", "arbitrary")),
        name="inproj",
    )(*args)


def _tri_dot_left(tri, a):
    hi, mid, lo = _split3(a)
    return _dot(tri, hi) + _dot(tri, mid) + _dot(tri, lo)


def _tri_dot_right(a, tri):
    hi, mid, lo = _split3(a)
    return _dot(hi, tri) + _dot(mid, tri) + _dot(lo, tri)


def _ssd_kernel(*refs, seq_len, has_h0, emit_state):
    it = iter(refs)
    x_ref, b_ref, c_ref, z_ref, dtc_ref, dtr_ref = (next(it) for _ in range(6))
    cwx_ref, cwb_ref, cwc_ref, cbx_ref, cbb_ref, cbc_ref = (next(it) for _ in range(6))
    prow_ref, pcol_ref, dsk_ref, nw_ref, expf_ref, expb_ref = (next(it) for _ in range(6))
    h0_ref = next(it) if has_h0 else None
    y_ref = next(it)
    st_ref = next(it) if emit_state else None
    pad_scr, cv_scr, yb_scr, dtc_scr, dtr_scr, hf_scr, hb_scr, m_scr, yd_scr = (next(it) for _ in range(9))

    L = seq_len
    nc = L // CHUNK
    W = GROUP_CH + 2 * SSD_STATE

    zeros8 = jnp.zeros((8, W), F32)
    pad_scr[0:8, :] = zeros8
    pad_scr[L + 8:L + 16, :] = zeros8

    def stage(c, carry):
        r0 = pl.multiple_of(c * CHUNK, CHUNK)
        pad_scr[pl.ds(r0 + 8, CHUNK), 0:GROUP_CH] = x_ref[0, pl.ds(r0, CHUNK), :].astype(F32)
        pad_scr[pl.ds(r0 + 8, CHUNK), GROUP_CH:GROUP_CH + SSD_STATE] = b_ref[0, pl.ds(r0, CHUNK), :].astype(F32)
        pad_scr[pl.ds(r0 + 8, CHUNK), GROUP_CH + SSD_STATE:W] = c_ref[0, pl.ds(r0, CHUNK), :].astype(F32)
        return carry

    lax.fori_loop(0, nc, stage, 0)

    cw = [cwx_ref[...][:, k * 128:(k + 1) * 128] for k in range(GROUP_CH // 128)] + [cwb_ref[...], cwc_ref[...]]
    cb = [cbx_ref[...][:, k * 128:(k + 1) * 128] for k in range(GROUP_CH // 128)] + [cbb_ref[...], cbc_ref[...]]
    win_rows = CHUNK + 16

    def conv(c, carry):
        r0 = pl.multiple_of(c * CHUNK, CHUNK)
        for k in range(W // 128):
            win = pad_scr[pl.ds(r0, win_rows), k * 128:(k + 1) * 128]
            acc = win[8:8 + CHUNK, :] * cw[k][2:3, :]
            for tap in (0, 1, 3, 4):
                d = tap - 2
                rolled = pltpu.roll(win, (win_rows - d) % win_rows, 0)
                acc = acc + rolled[8:8 + CHUNK, :] * cw[k][tap:tap + 1, :]
            acc = acc + cb[k]
            cv_scr[pl.ds(r0, CHUNK), k * 128:(k + 1) * 128] = _silu(acc)
        return carry

    lax.fori_loop(0, nc, conv, 0)

    bias_row = prow_ref[0, 0:1, :]
    aneg_row = -jnp.exp(prow_ref[0, 1:2, :])
    bias_col = pcol_ref[0, :, 0:1]
    aneg_col = -jnp.exp(pcol_ref[0, :, 1:2])
    dtc_scr[...] = _softplus(dtc_ref[0] + bias_row)
    for c in range(nc):
        dtr_scr[c] = _softplus(dtr_ref[0, :, c * CHUNK:(c + 1) * CHUNK] + bias_col)

    ri = lax.broadcasted_iota(jnp.int32, (CHUNK, CHUNK), 0)
    ci = lax.broadcasted_iota(jnp.int32, (CHUNK, CHUNK), 1)
    lower = ci <= ri
    tri_le = jnp.where(lower, 1.0, 0.0).astype(BF16)
    tri_ge = jnp.where(ci >= ri, 1.0, 0.0).astype(BF16)
    diag = ci == ri
    lane16 = lax.broadcasted_iota(jnp.int32, (CHUNK, 16), 1)
    row16 = lax.broadcasted_iota(jnp.int32, (16, CHUNK), 0)

    def col_scalars(r0):
        dt = dtc_scr[pl.ds(r0, CHUNK), :]
        a = dt * aneg_row
        acs = jnp.where(lane16 < SSD_HPG, _tri_dot_left(tri_le, a), _tri_dot_left(tri_ge, a))
        last = jnp.where(lane16[0:1, :] < SSD_HPG, acs[CHUNK - 1:CHUNK, :], acs[0:1, :])
        e = jnp.exp(acs)
        w = dt * jnp.exp(last - acs)
        return dt, acs, e, w

    def expand(e, w, exp_ref):
        e_hi, e_lo = _split2(e)
        w_hi, w_lo = _split2(w)
        m_scr[:, 0:16] = e_hi.astype(F32)
        m_scr[:, 16:32] = e_lo.astype(F32)
        m_scr[:, 32:48] = w_hi.astype(F32)
        m_scr[:, 48:64] = w_lo.astype(F32)
        full = _dot(m_scr[...].astype(BF16), exp_ref[...])
        return full[:, 0:GROUP_CH], full[:, GROUP_CH:2 * GROUP_CH]

    if has_h0:
        hf_scr[...] = h0_ref[0, 0, 0, 0].T
        hb_scr[...] = h0_ref[0, 0, 1, 0].T
    else:
        hf_scr[...] = jnp.zeros((SSD_STATE, GROUP_CH), F32)
        hb_scr[...] = jnp.zeros((SSD_STATE, GROUP_CH), F32)

    def bwd(i, carry):
        c = nc - 1 - i
        r0 = pl.multiple_of(c * CHUNK, CHUNK)
        xs = cv_scr[pl.ds(r0, CHUNK), 0:GROUP_CH]
        bm = cv_scr[pl.ds(r0, CHUNK), GROUP_CH:GROUP_CH + SSD_STATE]
        cm = cv_scr[pl.ds(r0, CHUNK), GROUP_CH + SSD_STATE:W]
        _, _, e, w = col_scalars(r0)
        e_x, w_x = expand(e, w, expb_ref)
        h = hb_scr[...]
        yb_scr[pl.ds(r0, CHUNK), :] = _dot(cm.astype(BF16), h.astype(BF16)) * e_x
        hb_scr[...] = h * e_x[0:1, :] + _dot(bm.T.astype(BF16), (xs * w_x).astype(BF16))
        return carry

    lax.fori_loop(0, nc, bwd, 0)

    dsk = dsk_ref[...]
    nw = nw_ref[...]

    def fwd(c, carry):
        r0 = pl.multiple_of(c * CHUNK, CHUNK)
        xs = cv_scr[pl.ds(r0, CHUNK), 0:GROUP_CH]
        bm = cv_scr[pl.ds(r0, CHUNK), GROUP_CH:GROUP_CH + SSD_STATE]
        cm = cv_scr[pl.ds(r0, CHUNK), GROUP_CH + SSD_STATE:W]
        dt, acs, e, w = col_scalars(r0)
        e_x, w_x = expand(e, w, expf_ref)

        dtr = dtr_scr[c]
        a_r = dtr * aneg_col
        acs_r = jnp.where(row16 < SSD_HPG, _tri_dot_right(a_r, tri_ge), _tri_dot_right(a_r, tri_le))
        q_r = acs_r - jnp.log(dtr)

        cmb = cm.astype(BF16)
        cb_mat = _dot_nt(cmb, bm.astype(BF16))
        for hh in range(SSD_HPG):
            pf = acs[:, hh:hh + 1]
            pb = acs[:, SSD_HPG + hh:SSD_HPG + hh + 1]
            qf = q_r[hh:hh + 1, :]
            qb = q_r[SSD_HPG + hh:SSD_HPG + hh + 1, :]
            dtb = dt[:, SSD_HPG + hh:SSD_HPG + hh + 1]
            arg = jnp.where(lower, pf - qf, pb - qb)
            wmat = cb_mat * (jnp.exp(arg) + jnp.where(diag, dtb, 0.0))
            yd_scr[:, hh * SSD_HEADDIM:(hh + 1) * SSD_HEADDIM] = _dot(
                wmat.astype(BF16), xs[:, hh * SSD_HEADDIM:(hh + 1) * SSD_HEADDIM].astype(BF16))

        h = hf_scr[...]
        y = yb_scr[pl.ds(r0, CHUNK), :] + _dot(cmb, h.astype(BF16)) * e_x + yd_scr[...] + xs * dsk
        hf_scr[...] = h * e_x[CHUNK - 1:CHUNK, :] + _dot(bm.T.astype(BF16), (xs * w_x).astype(BF16))

        zz = z_ref[0, pl.ds(r0, CHUNK), :].astype(F32)
        y_ref[0, pl.ds(r0, CHUNK), :] = (_rms(y * _silu(zz)) * nw).astype(BF16)
        return carry

    lax.fori_loop(0, nc, fwd, 0)

    if emit_state:
        st_ref[0, 0, 0] = hf_scr[...].T
        st_ref[0, 1, 0] = hb_scr[...].T


def _expander(first_row):
    m = np.zeros((64, 2 * GROUP_CH), np.float32)
    for blk in range(4):
        half = blk // 2
        for j in range(SSD_HPG):
            m[blk * 16 + first_row + j, half * GROUP_CH + j * SSD_HEADDIM: half * GROUP_CH + (j + 1) * SSD_HEADDIM] = 1.0
    return jnp.asarray(m, BF16)


def _ssd(proj, dtc, dtr, lw, h0, *, batch, seq_len, layer, emit_state):
    L = seq_len
    proj3 = proj.reshape(batch, L, MAIN_COLS)
    has_h0 = h0 is not None
    gc = GROUP_CH
    W = gc + 2 * SSD_STATE

    in_specs = [
        pl.BlockSpec((1, L, gc), lambda b, g: (b, 0, COL_X // gc + g)),
        pl.BlockSpec((1, L, SSD_STATE), lambda b, g: (b, 0, COL_B // SSD_STATE + g)),
        pl.BlockSpec((1, L, SSD_STATE), lambda b, g: (b, 0, COL_C // SSD_STATE + g)),
        pl.BlockSpec((1, L, gc), lambda b, g: (b, 0, COL_Z // gc + g)),
        pl.BlockSpec((1, L, 16), lambda b, g: (g, b, 0)),
        pl.BlockSpec((1, 16, L), lambda b, g: (g, 0, b)),
        pl.BlockSpec((SSD_CONV, gc), lambda b, g: (0, g)),
        pl.BlockSpec((SSD_CONV, SSD_STATE), lambda b, g: (0, SSD_INNER // SSD_STATE + g)),
        pl.BlockSpec((SSD_CONV, SSD_STATE), lambda b, g: (0, SSD_INNER // SSD_STATE + SSD_GROUPS + g)),
        pl.BlockSpec((1, gc), lambda b, g: (0, g)),
        pl.BlockSpec((1, SSD_STATE), lambda b, g: (0, SSD_INNER // SSD_STATE + g)),
        pl.BlockSpec((1, SSD_STATE), lambda b, g: (0, SSD_INNER // SSD_STATE + SSD_GROUPS + g)),
        pl.BlockSpec((1, 2, 16), lambda b, g: (g, 0, 0)),
        pl.BlockSpec((1, 16, 2), lambda b, g: (g, 0, 0)),
        pl.BlockSpec((1, gc), lambda b, g: (0, g)),
        pl.BlockSpec((1, gc), lambda b, g: (0, g)),
        pl.BlockSpec((64, 2 * gc), lambda b, g: (0, 0)),
        pl.BlockSpec((64, 2 * gc), lambda b, g: (0, 0)),
    ]
    args = [proj3, proj3, proj3, proj3, dtc, dtr,
            lw["conv_w"], lw["conv_w"], lw["conv_w"], lw["conv_b"], lw["conv_b"], lw["conv_b"],
            lw["ssd_prow"], lw["ssd_pcol"], lw["d_skip_x"], lw["ssd_norm_w"],
            _expander(0), _expander(SSD_HPG)]
    if has_h0:
        in_specs.append(pl.BlockSpec((1, 1, 2, 1, gc, SSD_STATE), lambda b, g: (b, layer, 0, g, 0, 0)))
        args.append(h0)
    out_specs = [pl.BlockSpec((1, L, gc), lambda b, g: (b, 0, g))]
    out_shape = [jax.ShapeDtypeStruct((batch, L, SSD_INNER), BF16)]
    if emit_state:
        out_specs.append(pl.BlockSpec((1, 2, 1, gc, SSD_STATE), lambda b, g: (b, 0, g, 0, 0)))
        out_shape.append(jax.ShapeDtypeStruct((batch, 2, SSD_GROUPS, gc, SSD_STATE), F32))
    nc = L // CHUNK
    scratch = [
        pltpu.VMEM((L + 16, W), F32),
        pltpu.VMEM((L, W), F32),
        pltpu.VMEM((L, gc), F32),
        pltpu.VMEM((L, 16), F32),
        pltpu.VMEM((nc, 16, CHUNK), F32),
        pltpu.VMEM((SSD_STATE, gc), F32),
        pltpu.VMEM((SSD_STATE, gc), F32),
        pltpu.VMEM((CHUNK, 64), F32),
        pltpu.VMEM((CHUNK, gc), F32),
    ]
    outs = pl.pallas_call(
        functools.partial(_ssd_kernel, seq_len=L, has_h0=has_h0, emit_state=emit_state),
        grid=(batch, SSD_GROUPS),
        in_specs=in_specs,
        out_specs=out_specs,
        out_shape=out_shape,
        scratch_shapes=scratch,
        compiler_params=_cparams(("arbitrary", "arbitrary")),
        name="ssd",
    )(*args)
    y = outs[0].reshape(batch * L, SSD_INNER)
    return y, (outs[1] if emit_state else None)


def _route(sel, scores):
    neg = -jnp.inf

    def first_max(vals):
        m = vals[0]
        for v in vals[1:]:
            m = jnp.maximum(m, v)
        taken = jnp.zeros_like(m)
        flags = []
        for v in vals:
            f = jnp.where(v == m, 1.0, 0.0) * (1.0 - taken)
            flags.append(f)
            taken = taken + f
        return m, flags

    group_scores = []
    for j in range(N_EXPERT_GROUPS):
        a = sel[j * EXPERTS_PER_GROUP:(j + 1) * EXPERTS_PER_GROUP]
        m1, f1 = first_max(a)
        m2, _ = first_max([jnp.where(f > 0.5, neg, v) for f, v in zip(f1, a)])
        group_scores.append(m1 + m2)
    _, gflag = first_max(group_scores)
    masked = [jnp.where(gflag[e // EXPERTS_PER_GROUP] > 0.5, sel[e], neg) for e in range(N_EXPERTS)]
    _, f1 = first_max(masked)
    _, f2 = first_max([jnp.where(f > 0.5, neg, v) for f, v in zip(f1, masked)])
    w1 = sum(f * s for f, s in zip(f1, scores))
    w2 = sum(f * s for f, s in zip(f2, scores))
    tot = w1 + w2
    gates = [(f1[e] * w1 + f2[e] * w2) / tot for e in range(N_EXPERTS)]
    in_group = [sum(gflag[j] * gates[j * EXPERTS_PER_GROUP + k] for j in range(N_EXPERT_GROUPS))
                for k in range(EXPERTS_PER_GROUP)]
    group_id = sum(float(j) * gflag[j] for j in range(1, N_EXPERT_GROUPS))
    return in_group, group_id


def _mixout_kernel(*refs, has_pos, tm, rows_per_mod):
    it = iter(refs)
    x_ref = next(it)
    pos_ref = next(it) if has_pos else None
    ga_ref, gb_ref, u_ref, v_ref, yb_ref, mod_ref = (next(it) for _ in range(6))
    gnw_ref, ws_ref, bsx_ref, wa_ref, wb_ref, wo_ref, n2w_ref, rwt_ref, rb_ref = (next(it) for _ in range(9))
    rows_ref, gid_ref = (next(it) for _ in range(2))
    g_scr = next(it)

    x = x_ref[...]
    if has_pos:
        x = x + pos_ref[...]

    u = _gelu_tanh(u_ref[...].astype(F32))
    vn = (_rms(_gelu_tanh(v_ref[...].astype(F32))) * gnw_ref[...]).astype(BF16)
    nch = tm // CHUNK
    gd = GMLP_WIDTH // GMLP_GROUPS
    mixed_cols = [[None] * GMLP_GROUPS for _ in range(nch)]
    for g in range(GMLP_GROUPS):
        rhs = jnp.concatenate([vn[c * CHUNK:(c + 1) * CHUNK, g * gd:(g + 1) * gd] for c in range(nch)], axis=1)
        res = _dot(ws_ref[g], rhs)
        for c in range(nch):
            mixed_cols[c][g] = res[:, c * gd:(c + 1) * gd]
    mixed = jnp.concatenate([jnp.concatenate(mixed_cols[c], axis=1) + bsx_ref[...] for c in range(nch)], axis=0)
    ya = (u * mixed).astype(BF16)

    merged = (jax.nn.sigmoid(ga_ref[...].astype(F32)) * _dot(ya, wa_ref[...])
              + jax.nn.sigmoid(gb_ref[...].astype(F32)) * _dot(yb_ref[...], wb_ref[...]))
    xn = x + mod_ref[0, 2:3, :] * _dot(merged.astype(BF16), wo_ref[...])
    rows_ref[:, ROW_X:ROW_X + D_MODEL] = xn

    h2 = (_rms(xn) * n2w_ref[...]) * (1.0 + mod_ref[0, 4:5, :]) + mod_ref[0, 3:4, :]
    rows_ref[:, ROW_H:ROW_H + D_MODEL] = h2

    h_hi, h_lo = _split2(h2)
    r_hi, r_lo = _split2(rwt_ref[...])
    logits = _dot_nt(r_hi, h_hi) + _dot_nt(r_hi, h_lo) + _dot_nt(r_lo, h_hi)
    scores = jax.nn.sigmoid(logits)
    selm = scores + rb_ref[...]
    in_group, group_id = _route([selm[e:e + 1, :] for e in range(N_EXPERTS)],
                                [scores[e:e + 1, :] for e in range(N_EXPERTS)])
    gid_ref[...] = group_id.astype(jnp.int32)
    g_scr[...] = jnp.zeros(g_scr.shape, F32)
    for k in range(EXPERTS_PER_GROUP):
        g_scr[k:k + 1, :] = in_group[k]
    mod_row = (pl.program_id(0) * tm) // rows_per_mod
    g_scr[AUX_MODROW:AUX_MODROW + 1, :] = jnp.full((1, tm), mod_row, jnp.int32).astype(F32)
    rows_ref[:, ROW_AUX:ROW_W] = g_scr[...].T


def _mixout(x, pos, proj, yb, mod, lw, *, n_tokens):
    t, d = n_tokens, x.shape[1]
    tm = 256
    nb = mod.shape[0]
    rows_per_mod = t // nb
    has_pos = pos is not None
    full = lambda shape: pl.BlockSpec(shape, lambda i: (0,) * len(shape))
    in_specs = [pl.BlockSpec((tm, d), lambda i: (i, 0))]
    args = [x]
    if has_pos:
        pos_blocks = pos.shape[0] // tm
        in_specs.append(pl.BlockSpec((tm, d), lambda i: (i % pos_blocks, 0)))
        args.append(pos)
    in_specs += [pl.BlockSpec((tm, d), lambda i: (i, 0)),
                 pl.BlockSpec((tm, d), lambda i: (i, 1)),
                 pl.BlockSpec((tm, d), lambda i: (i, 2)),
                 pl.BlockSpec((tm, d), lambda i: (i, 3)),
                 pl.BlockSpec((tm, SSD_INNER), lambda i: (i, 0)),
                 pl.BlockSpec((1, N_MOD, d), lambda i: ((i * tm) // rows_per_mod, 0, 0)),
                 full((1, GMLP_WIDTH)),
                 full((GMLP_GROUPS, CHUNK, CHUNK)),
                 full((CHUNK, GMLP_WIDTH)),
                 full((GMLP_WIDTH, d)),
                 full((SSD_INNER, d)),
                 full((d, d)),
                 full((1, d)),
                 full((N_EXPERTS, d)),
                 full((N_EXPERTS, 1))]
    args += [proj, proj, proj, proj, yb, mod,
             lw["gmlp_norm_w"], lw["gmlp_ws"], lw["gmlp_bs_x"], lw["w_branch_a"], lw["w_branch_b"], lw["w_out"],
             lw["norm2_w"], lw["router_wt"], lw["router_bias"]]
    return pl.pallas_call(
        functools.partial(_mixout_kernel, has_pos=has_pos, tm=tm, rows_per_mod=rows_per_mod),
        grid=(t // tm,),
        in_specs=in_specs,
        out_specs=[pl.BlockSpec((tm, ROW_W), lambda i: (i, 0)),
                   pl.BlockSpec((1, tm), lambda i: (0, i))],
        out_shape=[jax.ShapeDtypeStruct((t, ROW_W), F32),
                   jax.ShapeDtypeStruct((1, t), jnp.int32)],
        scratch_shapes=[pltpu.VMEM((ROW_W - ROW_AUX, tm), F32)],
        compiler_params=_cparams(("arbitrary",)),
        name="mixout",
    )(*args)


def _moe_plan(gid, n_tokens, tm):
    ng = N_EXPERT_GROUPS
    t = n_tokens
    sm = 2 * tm
    i32 = jnp.int32
    onehot = (gid[:, None] == jnp.arange(ng, dtype=i32)[None, :]).astype(i32)
    csum = jnp.cumsum(onehot, axis=0)
    rank = jnp.sum(onehot * (csum - 1), axis=1)
    counts = csum[-1]
    padded = ((counts + sm - 1) // sm) * sm
    ends = jnp.cumsum(padded)
    dest = jnp.sum(onehot * (ends - padded)[None, :], axis=1) + rank
    ns = pl.cdiv(t, sm) + ng
    tok = jnp.arange(t, dtype=i32)
    spare = t + jnp.arange((ns + 1) * sm, dtype=i32) % tm
    src = jnp.zeros(((ns + 1) * sm,), i32).at[dest].set(tok)
    dst = spare.at[dest + sm].set(tok)
    step_grp = jnp.sum((jnp.arange(ns, dtype=i32) * sm)[:, None] >= ends[None, :], axis=1)
    step_grp = jnp.minimum(step_grp, ng - 1).astype(i32)
    return src.reshape(ns + 1, 2, 1, tm), dst.reshape(ns + 1, 2, 1, tm), step_grp


def _moe_kernel(grp_ref, src_a0_ref, src_b_ref, src_an_ref, dst_bp_ref, dst_a_ref, dst_bl_ref,
                rows_hbm, g2_ref, wg_ref, wu_ref, wd_ref, fnw_ref,
                out_hbm, g0, g1, o0, o1, gsem, ssem, *, tm, final, n_mod):
    i = pl.program_id(0)
    ns = pl.num_programs(0)

    n_stage = 2 * EXPERTS_PER_GROUP
    per_stage = tm // n_stage

    def start_gather(idx_ref, buf, s, stage=None):
        rows = range(tm) if stage is None else range(stage * per_stage, (stage + 1) * per_stage)
        for r in rows:
            pltpu.make_async_copy(rows_hbm.at[pl.ds(idx_ref[0, 0, 0, r], 1), :], buf.at[pl.ds(r, 1), :],
                                  gsem.at[s]).start()

    def start_scatter(idx_ref, buf, s, stage=None):
        rows = range(tm) if stage is None else range(stage * per_stage, (stage + 1) * per_stage)
        for r in rows:
            pltpu.make_async_copy(buf.at[pl.ds(r, 1), :], out_hbm.at[pl.ds(idx_ref[0, 0, 0, r], 1), :],
                                  ssem.at[s]).start()

    def wait_gather(buf, s):
        pltpu.make_async_copy(rows_hbm.at[pl.ds(0, tm), :], buf, gsem.at[s]).wait()

    def wait_scatter(buf, s):
        pltpu.make_async_copy(buf, out_hbm.at[pl.ds(0, tm), :], ssem.at[s]).wait()

    def compute(buf, issue):
        rows = buf[...]
        h = rows[:, ROW_H:ROW_H + D_MODEL].astype(BF16)
        y = None
        for k in range(EXPERTS_PER_GROUP):
            issue(2 * k)
            act = _silu(_dot(h, wg_ref[k])) * _dot(h, wu_ref[k])
            gate = rows[:, ROW_AUX + k:ROW_AUX + k + 1]
            issue(2 * k + 1)
            part = _dot((act * gate).astype(BF16), wd_ref[k])
            y = part if y is None else y + part
        if n_mod == 1:
            g2 = g2_ref[0:1, :]
        else:
            mod_row = rows[:, ROW_AUX + AUX_MODROW:ROW_AUX + AUX_MODROW + 1]
            ids = lax.broadcasted_iota(jnp.int32, (tm, g2_ref.shape[0]), 1).astype(F32)
            onehot = jnp.where(mod_row == ids, 1.0, 0.0).astype(BF16)
            g_hi, g_lo = _split2(g2_ref[...])
            g2 = _dot(onehot, g_hi) + _dot(onehot, g_lo)
        xn = rows[:, ROW_X:ROW_X + D_MODEL] + g2 * y
        if final:
            xn = _rms(xn) * fnw_ref[...]
        return xn

    @pl.when(i == 0)
    def _():
        start_gather(src_a0_ref, g0, 0)
        o1[...] = jnp.zeros(o1.shape, F32)

    @pl.when(i > 0)
    def _():
        wait_scatter(o0, 0)

    def issue_a(stage):
        start_gather(src_b_ref, g1, 1, stage)
        start_scatter(dst_bp_ref, o1, 1, stage)

    def issue_b(stage):
        start_gather(src_an_ref, g0, 0, stage)
        start_scatter(dst_a_ref, o0, 0, stage)

    wait_gather(g0, 0)
    o0[...] = compute(g0, issue_a)

    wait_gather(g1, 1)
    wait_scatter(o1, 1)
    o1[...] = compute(g1, issue_b)

    @pl.when(i == ns - 1)
    def _():
        wait_scatter(o0, 0)
        start_scatter(dst_bl_ref, o1, 1)
        wait_scatter(o1, 1)
        wait_gather(g0, 0)


def _moe(rows, gid, g2, lw, final_norm_w, *, n_tokens, final):
    t, d, tm = n_tokens, D_MODEL, MOE_TILE
    src, dst, step_grp = _moe_plan(gid.reshape(t), t, tm)
    ns = step_grp.shape[0]
    n_mod = g2.shape[0]
    if n_mod > 1:
        g2 = jnp.pad(g2, ((0, (-n_mod) % 8), (0, 0)))
    eg = EXPERTS_PER_GROUP
    idx = lambda fn: pl.BlockSpec((1, 1, 1, tm), fn, memory_space=pltpu.SMEM)
    grid_spec = pltpu.PrefetchScalarGridSpec(
        num_scalar_prefetch=1,
        grid=(ns,),
        in_specs=[idx(lambda i, grp: (0, 0, 0, 0)),
                  idx(lambda i, grp: (i, 1, 0, 0)),
                  idx(lambda i, grp: (i + 1, 0, 0, 0)),
                  idx(lambda i, grp: (i, 1, 0, 0)),
                  idx(lambda i, grp: (i + 1, 0, 0, 0)),
                  idx(lambda i, grp: (ns, 1, 0, 0)),
                  pl.BlockSpec(memory_space=pl.ANY),
                  pl.BlockSpec(g2.shape, lambda i, grp: (0, 0)),
                  pl.BlockSpec((eg, d, D_EXPERT), lambda i, grp: (grp[i], 0, 0)),
                  pl.BlockSpec((eg, d, D_EXPERT), lambda i, grp: (grp[i], 0, 0)),
                  pl.BlockSpec((eg, D_EXPERT, d), lambda i, grp: (grp[i], 0, 0)),
                  pl.BlockSpec((1, d), lambda i, grp: (0, 0))],
        out_specs=pl.BlockSpec(memory_space=pl.ANY),
        scratch_shapes=[pltpu.VMEM((tm, ROW_W), F32), pltpu.VMEM((tm, ROW_W), F32),
                        pltpu.VMEM((tm, d), F32), pltpu.VMEM((tm, d), F32),
                        pltpu.SemaphoreType.DMA((2,)), pltpu.SemaphoreType.DMA((2,))])
    return pl.pallas_call(
        functools.partial(_moe_kernel, tm=tm, final=final, n_mod=n_mod),
        grid_spec=grid_spec,
        out_shape=jax.ShapeDtypeStruct((t + tm, d), F32),
        compiler_params=_cparams(("arbitrary",)),
        name="moe",
    )(step_grp, src, src, src, dst, dst, dst, rows, g2, lw["w_gate_e"], lw["w_up_e"], lw["w_down_e"],
      final_norm_w)


def _grid_pos_embed(n_tokens, dim):
    rows = n_tokens // GRID_W
    quarter = dim // 4
    omega = 1.0 / (10000.0 ** (jnp.arange(quarter, dtype=F32) / quarter))
    r = jnp.arange(rows, dtype=F32)[:, None] * omega
    col = jnp.arange(GRID_W, dtype=F32)[:, None] * omega
    r_emb = jnp.concatenate([jnp.sin(r), jnp.cos(r)], axis=-1)
    c_emb = jnp.concatenate([jnp.sin(col), jnp.cos(col)], axis=-1)
    emb = jnp.concatenate([
        jnp.broadcast_to(r_emb[:, None, :], (rows, GRID_W, dim // 2)),
        jnp.broadcast_to(c_emb[None, :, :], (rows, GRID_W, dim // 2))], axis=-1)
    return emb.reshape(rows * GRID_W, dim)


def _dt_perm():
    return np.array([dr * SSD_HEADS + g * SSD_HPG + j
                     for g in range(SSD_GROUPS) for dr in range(2) for j in range(SSD_HPG)], np.int32)


def _layer_weights(l, w_in, norm1_w, gmlp_norm_w, gmlp_ws, gmlp_bs, conv_w, conv_b, dt_bias, a_log, d_skip,
                   ssd_norm_w, w_branch_a, w_branch_b, w_out, norm2_w, router_w, router_bias,
                   w_gate_e, w_up_e, w_down_e):
    perm = _dt_perm()
    w_dt = jnp.pad(w_in[l][:, MAIN_COLS:][:, perm], ((0, 0), (0, DT_PAD - 2 * SSD_HEADS)))
    prow = jnp.stack([dt_bias[l].reshape(-1)[perm], a_log[l].reshape(-1)[perm]], axis=0)
    prow = prow.reshape(2, SSD_GROUPS, 16).transpose(1, 0, 2)
    return {
        "norm1_w": norm1_w[l][None],
        "w_main": w_in[l][:, :MAIN_COLS].astype(BF16),
        "w_dt": w_dt.astype(BF16),
        "gmlp_norm_w": gmlp_norm_w[l][None],
        "gmlp_ws": gmlp_ws[l].astype(BF16),
        "gmlp_bs_x": jnp.repeat(gmlp_bs[l].T, GMLP_WIDTH // GMLP_GROUPS, axis=1),
        "conv_w": conv_w[l],
        "conv_b": conv_b[l][None],
        "ssd_prow": prow,
        "ssd_pcol": prow.transpose(0, 2, 1),
        "d_skip_x": jnp.repeat(d_skip[l], SSD_HEADDIM)[None],
        "ssd_norm_w": ssd_norm_w[l][None],
        "w_branch_a": w_branch_a[l].astype(BF16),
        "w_branch_b": w_branch_b[l].astype(BF16),
        "w_out": w_out[l].astype(BF16),
        "norm2_w": norm2_w[l][None],
        "router_wt": router_w.T,
        "router_bias": router_bias[:, None],
        "w_gate_e": w_gate_e[l].astype(BF16),
        "w_up_e": w_up_e[l].astype(BF16),
        "w_down_e": w_down_e[l].astype(BF16),
    }


def _layer(x, pos, mod, lw, h0, final_norm_w, *, batch, seq_len, layer, emit_state, final):
    t = batch * seq_len
    proj, dtc, dtr = _inproj(x, pos, mod, lw["norm1_w"], lw["w_main"], lw["w_dt"], n_tokens=t)
    yb, state = _ssd(proj, dtc, dtr, lw, h0, batch=batch, seq_len=seq_len, layer=layer, emit_state=emit_state)
    rows, gid = _mixout(x, pos, proj, yb, mod, lw, n_tokens=t)
    out = _moe(rows, gid, mod[:, N_MOD - 1, :], lw, final_norm_w, n_tokens=t, final=final)
    return out, state


def kernel(x_prompt, x_sample, state_ssd, c, c_ctx, w_mod, b_mod, norm1_w, w_in, gmlp_norm_w, gmlp_ws, gmlp_bs, conv_w, conv_b, dt_bias, a_log, d_skip, ssd_norm_w, w_branch_a, w_branch_b, w_out, norm2_w, router_w, router_bias, w_gate_e, w_up_e, w_down_e, final_norm_w):
    bp, lp, d = x_prompt.shape
    bs, ls, _ = x_sample.shape
    depth = w_mod.shape[0]
    assert 1 + bs <= MOD_ROWS and d == D_MODEL
    assert lp % CHUNK == 0 and ls % CHUNK == 0

    cond = jnp.concatenate([c_ctx[None], c, jnp.zeros((MOD_ROWS - 1 - bs, d), F32)], axis=0)
    mod = _modulation(cond, w_mod, b_mod).reshape(depth, MOD_ROWS, N_MOD, d)

    pos = _grid_pos_embed(ls, d)
    xp = x_prompt.reshape(bp * lp, d)
    xs = x_sample.reshape(bs * ls, d)
    cache = state_ssd.reshape(bs, depth, 2, SSD_GROUPS, GROUP_CH, SSD_STATE)
    fnw = final_norm_w[None]

    states = []
    for l in range(depth):
        lw = _layer_weights(l, w_in, norm1_w, gmlp_norm_w, gmlp_ws, gmlp_bs, conv_w, conv_b, dt_bias, a_log,
                            d_skip, ssd_norm_w, w_branch_a, w_branch_b, w_out, norm2_w, router_w, router_bias,
                            w_gate_e, w_up_e, w_down_e)
        final = l == depth - 1
        xp, st = _layer(xp, None, mod[l, 0:1], lw, None, fnw, batch=bp, seq_len=lp, layer=l,
                        emit_state=True, final=final)
        states.append(st)
        xs, _ = _layer(xs, pos if l == 0 else None, mod[l, 1:1 + bs], lw, cache, fnw, batch=bs, seq_len=ls,
                       layer=l, emit_state=False, final=final)

    new_state = jnp.stack(states, axis=1).reshape(bp, depth, 2, SSD_HEADS, SSD_HEADDIM, SSD_STATE)
    return (xp[:bp * lp].reshape(bp, lp, d), xs[:bs * ls].reshape(bs, ls, d), new_state.astype(x_prompt.dtype))
```

```python
import functools
import math

import numpy as np
import jax
import jax.numpy as jnp
from jax import lax
from jax.experimental import pallas as pl
from jax.experimental.pallas import tpu as pltpu

F32 = jnp.float32
BF16 = jnp.bfloat16

D_MODEL = 1024
CHUNK = 128
GRID_W = 64
GMLP_WIDTH = 1024
GMLP_GROUPS = 8
SSD_INNER = 2048
SSD_HEADDIM = 64
SSD_HEADS = 32
SSD_GROUPS = 4
SSD_HPG = 8
SSD_STATE = 128
SSD_CONV = 5
GROUP_CH = SSD_INNER // SSD_GROUPS
N_EXPERTS = 16
N_EXPERT_GROUPS = 4
EXPERTS_PER_GROUP = 4
D_EXPERT = 512
N_MOD = 6
EPS = 1e-6
MAIN_COLS = 2 * D_MODEL + 2 * GMLP_WIDTH + SSD_INNER + SSD_INNER + 2 * SSD_GROUPS * SSD_STATE
COL_Z = 2 * D_MODEL + 2 * GMLP_WIDTH
COL_X = COL_Z + SSD_INNER
COL_B = COL_X + SSD_INNER
COL_C = COL_B + SSD_GROUPS * SSD_STATE
DT_PAD = 128
ROW_X = 0
ROW_H = D_MODEL
ROW_AUX = 2 * D_MODEL
ROW_W = 2 * D_MODEL + 128
AUX_MODROW = 4
MOE_TILE = 256
MOD_ROWS = 16

V7X_VMEM_LIMIT = 56 * 1024 * 1024


def _cparams(sem):
    return pltpu.CompilerParams(dimension_semantics=sem, vmem_limit_bytes=V7X_VMEM_LIMIT)


def _pick_tile(rows, preferred):
    tm = preferred
    while rows % tm:
        tm //= 2
    assert tm >= CHUNK
    return tm


def _split3(a):
    hi = a.astype(BF16)
    r1 = a - hi.astype(F32)
    mid = r1.astype(BF16)
    lo = (r1 - mid.astype(F32)).astype(BF16)
    return hi, mid, lo


def _split2(a):
    hi = a.astype(BF16)
    lo = (a - hi.astype(F32)).astype(BF16)
    return hi, lo


def _dot(a, b):
    return jnp.dot(a, b, preferred_element_type=F32)


def _dot_nt(a, b):
    return lax.dot_general(a, b, (((1,), (1,)), ((), ())), preferred_element_type=F32)


def _silu(x):
    return x * jax.nn.sigmoid(x)


def _gelu_tanh(x):
    c = math.sqrt(2.0 / math.pi)
    return x * (0.5 * (1.0 + jnp.tanh(c * (x + 0.044715 * (x * x * x)))))


def _softplus(x):
    return jnp.maximum(x, 0.0) + jnp.log1p(jnp.exp(-jnp.abs(x)))


def _rms(x):
    return x * lax.rsqrt(jnp.mean(x * x, axis=-1, keepdims=True) + EPS)


def _mod_kernel(cond_ref, w_ref, b_ref, o_ref):
    c = cond_ref[...]
    s_hi, s_lo = _split2(_silu(c))
    w_hi, w_lo = _split2(w_ref[0])
    o_ref[0] = _dot(s_hi, w_hi) + _dot(s_hi, w_lo) + _dot(s_lo, w_hi) + b_ref[0]


def _modulation(cond, w_mod, b_mod):
    depth, d, n = w_mod.shape
    tn = 1536
    return pl.pallas_call(
        _mod_kernel,
        grid=(depth, n // tn),
        in_specs=[pl.BlockSpec((MOD_ROWS, d), lambda l, j: (0, 0)),
                  pl.BlockSpec((1, d, tn), lambda l, j: (l, 0, j)),
                  pl.BlockSpec((1, 1, tn), lambda l, j: (l, 0, j))],
        out_specs=pl.BlockSpec((1, MOD_ROWS, tn), lambda l, j: (l, 0, j)),
        out_shape=jax.ShapeDtypeStruct((depth, MOD_ROWS, n), F32),
        compiler_params=_cparams(("arbitrary", "arbitrary")),
        name="modulation",
    )(cond, w_mod, b_mod.reshape(depth, 1, n))


def _inproj_kernel(*refs, has_pos, tm):
    if has_pos:
        x_ref, pos_ref, mod_ref, nw_ref, w_ref, wdt_ref, proj_ref, dtr_ref, h_scr = refs
    else:
        x_ref, mod_ref, nw_ref, w_ref, wdt_ref, proj_ref, dtr_ref, h_scr = refs
        pos_ref = None

    @pl.when(pl.program_id(1) == 0)
    def _():
        x = x_ref[...]
        if has_pos:
            x = x + pos_ref[...]
        h = (_rms(x) * nw_ref[...]) * (1.0 + mod_ref[0, 1:2, :]) + mod_ref[0, 0:1, :]
        hb = h.astype(BF16)
        h_scr[...] = hb
        dt = _dot(hb, wdt_ref[...])
        dtt = dt.T
        for g in range(SSD_GROUPS):
            dtr_ref[g] = dtt[g * 16:(g + 1) * 16, :]

    proj_ref[...] = _dot(h_scr[...], w_ref[...]).astype(BF16)


def _inproj(x, pos, mod, norm_w, w_main, w_dt, *, n_tokens):
    t, d = n_tokens, x.shape[1]
    tn = 1536
    nb = mod.shape[0]
    rows_per_mod = t // nb
    tm = _pick_tile(rows_per_mod, 1024)
    has_pos = pos is not None
    in_specs = [pl.BlockSpec((tm, d), lambda i, j: (i, 0))]
    args = [x]
    if has_pos:
        pos_blocks = pos.shape[0] // tm
        in_specs.append(pl.BlockSpec((tm, d), lambda i, j: (i % pos_blocks, 0)))
        args.append(pos)
    in_specs += [pl.BlockSpec((1, N_MOD, d), lambda i, j: ((i * tm) // rows_per_mod, 0, 0)),
                 pl.BlockSpec((1, d), lambda i, j: (0, 0)),
                 pl.BlockSpec((d, tn), lambda i, j: (0, j)),
                 pl.BlockSpec((d, DT_PAD), lambda i, j: (0, 0))]
    args += [mod, norm_w, w_main, w_dt]
    return pl.pallas_call(
        functools.partial(_inproj_kernel, has_pos=has_pos, tm=tm),
        grid=(t // tm, MAIN_COLS // tn),
        in_specs=in_specs,
        out_specs=[pl.BlockSpec((tm, tn), lambda i, j: (i, j)),
                   pl.BlockSpec((SSD_GROUPS, 16, tm), lambda i, j: (0, 0, i))],
        out_shape=[jax.ShapeDtypeStruct((t, MAIN_COLS), BF16),
                   jax.ShapeDtypeStruct((SSD_GROUPS, 16, t), F32)],
        scratch_shapes=[pltpu.VMEM((tm, d), BF16)],
        compiler_params=_cparams(("arbitrary", "arbitrary")),
        name="inproj",
    )(*args)


def _tri_dot_right(a, tri):
    hi, mid, lo = _split3(a)
    return _dot(hi, tri) + _dot(mid, tri) + _dot(lo, tri)


COL_ACS = 0
COL_E = 16
COL_W = 32
ROW_Q = 0
ROW_LD = 16


def _ssd_kernel(*refs, seq_len, has_h0, emit_state):
    it = iter(refs)
    x_ref, b_ref, c_ref, z_ref, dtr_ref = (next(it) for _ in range(5))
    cwx_ref, cwb_ref, cwc_ref, cbx_ref, cbb_ref, cbc_ref = (next(it) for _ in range(6))
    pcol_ref, dsk_ref, nw_ref, fanf_ref, fanb_ref = (next(it) for _ in range(5))
    h0_ref = next(it) if has_h0 else None
    y_ref = next(it)
    st_ref = next(it) if emit_state else None
    (pad_scr, xs_scr, cbf_scr, bbf_scr, btf_scr, yb_scr, yf_scr, dall_scr, acs_scr, e_scr, w_scr, q_scr, ld_scr,
     rt_scr, col_scr, rq_scr, hf_scr, hb_scr, xbd_scr, yd_scr) = (next(it) for _ in range(20))

    L = seq_len
    nc = L // CHUNK
    W = GROUP_CH + 2 * SSD_STATE
    n_xblk = GROUP_CH // 128

    zeros8 = jnp.zeros((8, W), F32)
    pad_scr[0:8, :] = zeros8
    pad_scr[L + 8:L + 16, :] = zeros8

    def stage(c, carry):
        r0 = pl.multiple_of(c * CHUNK, CHUNK)
        pad_scr[pl.ds(r0 + 8, CHUNK), 0:GROUP_CH] = x_ref[0, pl.ds(r0, CHUNK), :].astype(F32)
        pad_scr[pl.ds(r0 + 8, CHUNK), GROUP_CH:GROUP_CH + SSD_STATE] = b_ref[0, pl.ds(r0, CHUNK), :].astype(F32)
        pad_scr[pl.ds(r0 + 8, CHUNK), GROUP_CH + SSD_STATE:W] = c_ref[0, pl.ds(r0, CHUNK), :].astype(F32)
        return carry

    lax.fori_loop(0, nc, stage, 0)

    cw = [cwx_ref[...][:, k * 128:(k + 1) * 128] for k in range(GROUP_CH // 128)] + [cwb_ref[...], cwc_ref[...]]
    cb = [cbx_ref[...][:, k * 128:(k + 1) * 128] for k in range(GROUP_CH // 128)] + [cbb_ref[...], cbc_ref[...]]
    win_rows = CHUNK + 16

    def conv(c, carry):
        r0 = pl.multiple_of(c * CHUNK, CHUNK)
        for k in range(W // 128):
            win = pad_scr[pl.ds(r0, win_rows), k * 128:(k + 1) * 128]
            acc = win[8:8 + CHUNK, :] * cw[k][2:3, :]
            for tap in (0, 1, 3, 4):
                d = tap - 2
                rolled = pltpu.roll(win, (win_rows - d) % win_rows, 0)
                acc = acc + rolled[8:8 + CHUNK, :] * cw[k][tap:tap + 1, :]
            acc = _silu(acc + cb[k])
            if k < n_xblk:
                xs_scr[pl.ds(r0, CHUNK), k * 128:(k + 1) * 128] = acc
            elif k == n_xblk:
                bbf_scr[pl.ds(r0, CHUNK), :] = acc.astype(BF16)
                btf_scr[c] = acc.T.astype(BF16)
            else:
                cbf_scr[pl.ds(r0, CHUNK), :] = acc.astype(BF16)
        return carry

    lax.fori_loop(0, nc, conv, 0)

    ri = lax.broadcasted_iota(jnp.int32, (CHUNK, CHUNK), 0)
    ci = lax.broadcasted_iota(jnp.int32, (CHUNK, CHUNK), 1)
    below = ci < ri
    above = ci > ri
    tri_ge = jnp.where(ci >= ri, 1.0, 0.0).astype(BF16)
    tri_le = jnp.where(ci <= ri, 1.0, 0.0).astype(BF16)
    left_half = ci < SSD_HEADDIM

    bias_col = pcol_ref[0, :, 0:1]
    aneg_col = -jnp.exp(pcol_ref[0, :, 1:2])
    for c in range(nc):
        dall_scr[c * 16:(c + 1) * 16, :] = _softplus(dtr_ref[0, :, c * CHUNK:(c + 1) * CHUNK] + bias_col)
    d_all = dall_scr[...]
    nr = nc * 16
    a_all = d_all * jnp.concatenate([aneg_col] * nc, axis=0)
    fwd_rows = (lax.broadcasted_iota(jnp.int32, (nr, 1), 0) & SSD_HPG) == 0
    acs = jnp.where(fwd_rows, _tri_dot_right(a_all, tri_ge), _tri_dot_right(a_all, tri_le))
    total = jnp.where(fwd_rows, acs[:, CHUNK - 1:CHUNK], acs[:, 0:1])
    acs_scr[...] = acs
    e_scr[...] = jnp.exp(acs)
    w_scr[...] = d_all * jnp.exp(total - acs)
    q_scr[...] = acs - jnp.log(d_all)
    ld_scr[...] = jnp.log(d_all + pltpu.roll(d_all, nr - SSD_HPG, 0))

    rt_scr[...] = jnp.zeros(rt_scr.shape, F32)
    for c in range(nc):
        rows = slice(c * 16, (c + 1) * 16)
        rt_scr[COL_ACS:COL_ACS + 16, :] = acs_scr[rows, :]
        rt_scr[COL_E:COL_E + 16, :] = e_scr[rows, :]
        rt_scr[COL_W:COL_W + 16, :] = w_scr[rows, :]
        col_scr[c] = rt_scr[...].T
        rq_scr[c, ROW_Q:ROW_Q + 16, :] = q_scr[rows, :]
        rq_scr[c, ROW_LD:ROW_LD + SSD_HPG, :] = ld_scr[c * 16:c * 16 + SSD_HPG, :]

    def fans(col, fan_ref):
        hi, lo = _split2(col)
        both = _dot(jnp.concatenate([hi, lo], axis=1), fan_ref[...])
        return both[:, 0:GROUP_CH], both[:, GROUP_CH:2 * GROUP_CH]

    if has_h0:
        hf_scr[...] = h0_ref[0, 0, 0, 0].T
        hb_scr[...] = h0_ref[0, 0, 1, 0].T
    else:
        hf_scr[...] = jnp.zeros((SSD_STATE, GROUP_CH), F32)
        hb_scr[...] = jnp.zeros((SSD_STATE, GROUP_CH), F32)
    xbd_scr[...] = jnp.zeros(xbd_scr.shape, BF16)

    n_half = GROUP_CH // 256

    def sweeps(i, carry):
        passes = ((nc - 1 - i, hb_scr, fanb_ref, yb_scr, 0), (i, hf_scr, fanf_ref, yf_scr, CHUNK - 1))
        staged = []
        for c, h_scr, fan_ref, y_scr, total_row in passes:
            r0 = pl.multiple_of(c * CHUNK, CHUNK)
            e_x, w_x = fans(col_scr[c], fan_ref)
            staged.append((r0, cbf_scr[pl.ds(r0, CHUNK), :], btf_scr[c], e_x, w_x))
        for hf in range(n_half):
            sl = slice(hf * 256, (hf + 1) * 256)
            for (c, h_scr, fan_ref, y_scr, total_row), (r0, cm, bt, e_x, w_x) in zip(passes, staged):
                e2 = e_x[:, sl]
                h = h_scr[:, sl]
                y_scr[pl.ds(r0, CHUNK), sl] = _dot(cm, h.astype(BF16)) * e2
                xw = (xs_scr[pl.ds(r0, CHUNK), sl] * w_x[:, sl]).astype(BF16)
                h_scr[:, sl] = h * e2[total_row:total_row + 1, :] + _dot(bt, xw)
        return carry

    lax.fori_loop(0, nc, sweeps, 0)

    def finish(j, carry):
        work = []
        for u in range(2):
            c = 2 * j + u
            r0 = pl.multiple_of(c * CHUNK, CHUNK)
            cb_mat = _dot_nt(cbf_scr[pl.ds(r0, CHUNK), :], bbf_scr[pl.ds(r0, CHUNK), :])
            work.append((u, r0, col_scr[c], rq_scr[c], cb_mat))

        for u, r0, col, rq, cb_mat in work:
            for p in range(n_xblk):
                xb = xs_scr[pl.ds(r0, CHUNK), p * 128:(p + 1) * 128].astype(BF16)
                xbd_scr[u, p, 0:CHUNK, 0:SSD_HEADDIM] = xb[:, 0:SSD_HEADDIM]
                xbd_scr[u, p, CHUNK:2 * CHUNK, SSD_HEADDIM:2 * SSD_HEADDIM] = xb[:, SSD_HEADDIM:2 * SSD_HEADDIM]
        for p in range(n_xblk):
            for u, r0, col, rq, cb_mat in work:
                wpair = []
                for hh in (2 * p, 2 * p + 1):
                    pf = col[:, COL_ACS + hh:COL_ACS + hh + 1]
                    pb = col[:, COL_ACS + SSD_HPG + hh:COL_ACS + SSD_HPG + hh + 1]
                    qf = rq[ROW_Q + hh:ROW_Q + hh + 1, :]
                    qb = rq[ROW_Q + SSD_HPG + hh:ROW_Q + SSD_HPG + hh + 1, :]
                    ld = rq[ROW_LD + hh:ROW_LD + hh + 1, :]
                    arg = jnp.where(below, pf - qf, jnp.where(above, pb - qb, ld))
                    wpair.append((cb_mat * jnp.exp(arg)).astype(BF16))
                yd_scr[u, :, p * 128:(p + 1) * 128] = _dot(jnp.concatenate(wpair, axis=1), xbd_scr[u, p])

        for u, r0, col, rq, cb_mat in work:
            ssq = jnp.zeros((CHUNK, 1), F32)
            for hf in range(n_half):
                sl = slice(hf * 256, (hf + 1) * 256)
                y = (yb_scr[pl.ds(r0, CHUNK), sl] + yf_scr[pl.ds(r0, CHUNK), sl] + yd_scr[u, :, sl]
                     + xs_scr[pl.ds(r0, CHUNK), sl] * dsk_ref[:, sl])
                yg = y * _silu(z_ref[0, pl.ds(r0, CHUNK), sl].astype(F32))
                yd_scr[u, :, sl] = yg
                ssq = ssq + jnp.sum(yg * yg, axis=1, keepdims=True)
            scale = lax.rsqrt(ssq * (1.0 / GROUP_CH) + EPS)
            for hf in range(n_half):
                sl = slice(hf * 256, (hf + 1) * 256)
                y_ref[0, pl.ds(r0, CHUNK), sl] = (yd_scr[u, :, sl] * scale * nw_ref[:, sl]).astype(BF16)
        return carry

    lax.fori_loop(0, nc // 2, finish, 0)

    if emit_state:
        st_ref[0, 0, 0] = hf_scr[...].T
        st_ref[0, 1, 0] = hb_scr[...].T


def _fan_matrix(first_head):
    m = np.zeros((2 * CHUNK, 2 * GROUP_CH), np.float32)
    for piece in range(2):
        for part, base in enumerate((COL_E, COL_W)):
            for j in range(SSD_HPG):
                m[piece * CHUNK + base + first_head + j,
                  part * GROUP_CH + j * SSD_HEADDIM:part * GROUP_CH + (j + 1) * SSD_HEADDIM] = 1.0
    return jnp.asarray(m, BF16)


def _ssd(proj, dtr, lw, h0, *, batch, seq_len, layer, emit_state):
    L = seq_len
    proj3 = proj.reshape(batch, L, MAIN_COLS)
    has_h0 = h0 is not None
    gc = GROUP_CH
    W = gc + 2 * SSD_STATE

    in_specs = [
        pl.BlockSpec((1, L, gc), lambda b, g: (b, 0, COL_X // gc + g)),
        pl.BlockSpec((1, L, SSD_STATE), lambda b, g: (b, 0, COL_B // SSD_STATE + g)),
        pl.BlockSpec((1, L, SSD_STATE), lambda b, g: (b, 0, COL_C // SSD_STATE + g)),
        pl.BlockSpec((1, L, gc), lambda b, g: (b, 0, COL_Z // gc + g)),
        pl.BlockSpec((1, 16, L), lambda b, g: (g, 0, b)),
        pl.BlockSpec((SSD_CONV, gc), lambda b, g: (0, g)),
        pl.BlockSpec((SSD_CONV, SSD_STATE), lambda b, g: (0, SSD_INNER // SSD_STATE + g)),
        pl.BlockSpec((SSD_CONV, SSD_STATE), lambda b, g: (0, SSD_INNER // SSD_STATE + SSD_GROUPS + g)),
        pl.BlockSpec((1, gc), lambda b, g: (0, g)),
        pl.BlockSpec((1, SSD_STATE), lambda b, g: (0, SSD_INNER // SSD_STATE + g)),
        pl.BlockSpec((1, SSD_STATE), lambda b, g: (0, SSD_INNER // SSD_STATE + SSD_GROUPS + g)),
        pl.BlockSpec((1, 16, 2), lambda b, g: (g, 0, 0)),
        pl.BlockSpec((1, gc), lambda b, g: (0, g)),
        pl.BlockSpec((1, gc), lambda b, g: (0, g)),
        pl.BlockSpec((2 * CHUNK, 2 * gc), lambda b, g: (0, 0)),
        pl.BlockSpec((2 * CHUNK, 2 * gc), lambda b, g: (0, 0)),
    ]
    args = [proj3, proj3, proj3, proj3, dtr,
            lw["conv_w"], lw["conv_w"], lw["conv_w"], lw["conv_b"], lw["conv_b"], lw["conv_b"],
            lw["ssd_pcol"], lw["d_skip_x"], lw["ssd_norm_w"], _fan_matrix(0), _fan_matrix(SSD_HPG)]
    if has_h0:
        in_specs.append(pl.BlockSpec((1, 1, 2, 1, gc, SSD_STATE), lambda b, g: (b, layer, 0, g, 0, 0)))
        args.append(h0)
    out_specs = [pl.BlockSpec((1, L, gc), lambda b, g: (b, 0, g))]
    out_shape = [jax.ShapeDtypeStruct((batch, L, SSD_INNER), BF16)]
    if emit_state:
        out_specs.append(pl.BlockSpec((1, 2, 1, gc, SSD_STATE), lambda b, g: (b, 0, g, 0, 0)))
        out_shape.append(jax.ShapeDtypeStruct((batch, 2, SSD_GROUPS, gc, SSD_STATE), F32))
    nc = L // CHUNK
    scratch = [
        pltpu.VMEM((L + 16, W), F32),
        pltpu.VMEM((L, gc), F32),
        pltpu.VMEM((L, SSD_STATE), BF16),
        pltpu.VMEM((L, SSD_STATE), BF16),
        pltpu.VMEM((nc, SSD_STATE, CHUNK), BF16),
        pltpu.VMEM((L, gc), F32),
        pltpu.VMEM((L, gc), F32),
        pltpu.VMEM((nc * 16, CHUNK), F32),
        pltpu.VMEM((nc * 16, CHUNK), F32),
        pltpu.VMEM((nc * 16, CHUNK), F32),
        pltpu.VMEM((nc * 16, CHUNK), F32),
        pltpu.VMEM((nc * 16, CHUNK), F32),
        pltpu.VMEM((nc * 16, CHUNK), F32),
        pltpu.VMEM((CHUNK, CHUNK), F32),
        pltpu.VMEM((nc, CHUNK, CHUNK), F32),
        pltpu.VMEM((nc, 24, CHUNK), F32),
        pltpu.VMEM((SSD_STATE, gc), F32),
        pltpu.VMEM((SSD_STATE, gc), F32),
        pltpu.VMEM((2, gc // 128, 2 * CHUNK, 128), BF16),
        pltpu.VMEM((2, CHUNK, gc), F32),
    ]
    outs = pl.pallas_call(
        functools.partial(_ssd_kernel, seq_len=L, has_h0=has_h0, emit_state=emit_state),
        grid=(batch, SSD_GROUPS),
        in_specs=in_specs,
        out_specs=out_specs,
        out_shape=out_shape,
        scratch_shapes=scratch,
        compiler_params=_cparams(("arbitrary", "arbitrary")),
        name="ssd",
    )(*args)
    y = outs[0].reshape(batch * L, SSD_INNER)
    return y, (outs[1] if emit_state else None)


def _route(sel, scores):
    neg = -jnp.inf

    def first_max(vals):
        m = vals[0]
        for v in vals[1:]:
            m = jnp.maximum(m, v)
        taken = jnp.zeros_like(m)
        flags = []
        for v in vals:
            f = jnp.where(v == m, 1.0, 0.0) * (1.0 - taken)
            flags.append(f)
            taken = taken + f
        return m, flags

    group_scores = []
    for j in range(N_EXPERT_GROUPS):
        a = sel[j * EXPERTS_PER_GROUP:(j + 1) * EXPERTS_PER_GROUP]
        m1, f1 = first_max(a)
        m2, _ = first_max([jnp.where(f > 0.5, neg, v) for f, v in zip(f1, a)])
        group_scores.append(m1 + m2)
    _, gflag = first_max(group_scores)
    masked = [jnp.where(gflag[e // EXPERTS_PER_GROUP] > 0.5, sel[e], neg) for e in range(N_EXPERTS)]
    _, f1 = first_max(masked)
    _, f2 = first_max([jnp.where(f > 0.5, neg, v) for f, v in zip(f1, masked)])
    w1 = sum(f * s for f, s in zip(f1, scores))
    w2 = sum(f * s for f, s in zip(f2, scores))
    tot = w1 + w2
    gates = [(f1[e] * w1 + f2[e] * w2) / tot for e in range(N_EXPERTS)]
    in_group = [sum(gflag[j] * gates[j * EXPERTS_PER_GROUP + k] for j in range(N_EXPERT_GROUPS))
                for k in range(EXPERTS_PER_GROUP)]
    group_id = sum(float(j) * gflag[j] for j in range(1, N_EXPERT_GROUPS))
    return in_group, group_id


def _mixout_kernel(*refs, has_pos, tm, rows_per_mod):
    it = iter(refs)
    x_ref = next(it)
    pos_ref = next(it) if has_pos else None
    ga_ref, gb_ref, u_ref, v_ref, yb_ref, mod_ref = (next(it) for _ in range(6))
    gnw_ref, ws_ref, bsx_ref, wa_ref, wb_ref, wo_ref, n2w_ref, rwt_ref, rb_ref = (next(it) for _ in range(9))
    rows_ref, gid_ref = (next(it) for _ in range(2))
    g_scr = next(it)

    x = x_ref[...]
    if has_pos:
        x = x + pos_ref[...]

    u = _gelu_tanh(u_ref[...].astype(F32))
    vn = (_rms(_gelu_tanh(v_ref[...].astype(F32))) * gnw_ref[...]).astype(BF16)
    nch = tm // CHUNK
    gd = GMLP_WIDTH // GMLP_GROUPS
    mixed_cols = [[None] * GMLP_GROUPS for _ in range(nch)]
    for g in range(GMLP_GROUPS):
        rhs = jnp.concatenate([vn[c * CHUNK:(c + 1) * CHUNK, g * gd:(g + 1) * gd] for c in range(nch)], axis=1)
        res = _dot(ws_ref[g], rhs)
        for c in range(nch):
            mixed_cols[c][g] = res[:, c * gd:(c + 1) * gd]
    mixed = jnp.concatenate([jnp.concatenate(mixed_cols[c], axis=1) + bsx_ref[...] for c in range(nch)], axis=0)
    ya = (u * mixed).astype(BF16)

    merged = (jax.nn.sigmoid(ga_ref[...].astype(F32)) * _dot(ya, wa_ref[...])
              + jax.nn.sigmoid(gb_ref[...].astype(F32)) * _dot(yb_ref[...], wb_ref[...]))
    xn = x + mod_ref[0, 2:3, :] * _dot(merged.astype(BF16), wo_ref[...])
    rows_ref[:, ROW_X:ROW_X + D_MODEL] = xn

    h2 = (_rms(xn) * n2w_ref[...]) * (1.0 + mod_ref[0, 4:5, :]) + mod_ref[0, 3:4, :]
    rows_ref[:, ROW_H:ROW_H + D_MODEL] = h2

    h_hi, h_lo = _split2(h2)
    r_hi, r_lo = _split2(rwt_ref[...])
    logits = _dot_nt(r_hi, h_hi) + _dot_nt(r_hi, h_lo) + _dot_nt(r_lo, h_hi)
    scores = jax.nn.sigmoid(logits)
    selm = scores + rb_ref[...]
    in_group, group_id = _route([selm[e:e + 1, :] for e in range(N_EXPERTS)],
                                [scores[e:e + 1, :] for e in range(N_EXPERTS)])
    gid_ref[...] = group_id.astype(jnp.int32)
    g_scr[...] = jnp.zeros(g_scr.shape, F32)
    for k in range(EXPERTS_PER_GROUP):
        g_scr[k:k + 1, :] = in_group[k]
    mod_row = (pl.program_id(0) * tm) // rows_per_mod
    g_scr[AUX_MODROW:AUX_MODROW + 1, :] = jnp.full((1, tm), mod_row, jnp.int32).astype(F32)
    rows_ref[:, ROW_AUX:ROW_W] = g_scr[...].T


def _mixout(x, pos, proj, yb, mod, lw, *, n_tokens):
    t, d = n_tokens, x.shape[1]
    tm = 256
    nb = mod.shape[0]
    rows_per_mod = t // nb
    has_pos = pos is not None
    full = lambda shape: pl.BlockSpec(shape, lambda i: (0,) * len(shape))
    in_specs = [pl.BlockSpec((tm, d), lambda i: (i, 0))]
    args = [x]
    if has_pos:
        pos_blocks = pos.shape[0] // tm
        in_specs.append(pl.BlockSpec((tm, d), lambda i: (i % pos_blocks, 0)))
        args.append(pos)
    in_specs += [pl.BlockSpec((tm, d), lambda i: (i, 0)),
                 pl.BlockSpec((tm, d), lambda i: (i, 1)),
                 pl.BlockSpec((tm, d), lambda i: (i, 2)),
                 pl.BlockSpec((tm, d), lambda i: (i, 3)),
                 pl.BlockSpec((tm, SSD_INNER), lambda i: (i, 0)),
                 pl.BlockSpec((1, N_MOD, d), lambda i: ((i * tm) // rows_per_mod, 0, 0)),
                 full((1, GMLP_WIDTH)),
                 full((GMLP_GROUPS, CHUNK, CHUNK)),
                 full((CHUNK, GMLP_WIDTH)),
                 full((GMLP_WIDTH, d)),
                 full((SSD_INNER, d)),
                 full((d, d)),
                 full((1, d)),
                 full((N_EXPERTS, d)),
                 full((N_EXPERTS, 1))]
    args += [proj, proj, proj, proj, yb, mod,
             lw["gmlp_norm_w"], lw["gmlp_ws"], lw["gmlp_bs_x"], lw["w_branch_a"], lw["w_branch_b"], lw["w_out"],
             lw["norm2_w"], lw["router_wt"], lw["router_bias"]]
    return pl.pallas_call(
        functools.partial(_mixout_kernel, has_pos=has_pos, tm=tm, rows_per_mod=rows_per_mod),
        grid=(t // tm,),
        in_specs=in_specs,
        out_specs=[pl.BlockSpec((tm, ROW_W), lambda i: (i, 0)),
                   pl.BlockSpec((1, tm), lambda i: (0, i))],
        out_shape=[jax.ShapeDtypeStruct((t, ROW_W), F32),
                   jax.ShapeDtypeStruct((1, t), jnp.int32)],
        scratch_shapes=[pltpu.VMEM((ROW_W - ROW_AUX, tm), F32)],
        compiler_params=_cparams(("arbitrary",)),
        name="mixout",
    )(*args)


def _moe_plan(gid, n_tokens, tm):
    ng = N_EXPERT_GROUPS
    t = n_tokens
    i32 = jnp.int32
    onehot = (gid[:, None] == jnp.arange(ng, dtype=i32)[None, :]).astype(i32)
    csum = jnp.cumsum(onehot, axis=0)
    rank = jnp.sum(onehot * (csum - 1), axis=1)
    counts = csum[-1]
    padded = ((counts + tm - 1) // tm) * tm
    ends = jnp.cumsum(padded)
    dest = jnp.sum(onehot * (ends - padded)[None, :], axis=1) + rank
    nt = pl.cdiv(t, tm) + ng
    tile_grp = jnp.sum((jnp.arange(nt, dtype=i32) * tm)[:, None] >= ends[None, :], axis=1)
    tile_grp = jnp.minimum(tile_grp, ng - 1).astype(i32)
    n_used = (ends[-1] // tm).astype(i32).reshape(1)
    fill = jnp.concatenate([jnp.maximum(ends - tm, 0), (padded > 0).astype(i32), n_used,
                            jnp.full((1,), nt, i32)]).astype(i32)
    return dest.astype(i32), tile_grp, n_used, fill


def _permute_kernel(*refs, tn, scatter, fill_rows):
    if fill_rows:
        fill_ref, idx_ref, src_ref, out_ref, zbuf, sem, zsem = refs
    else:
        idx_ref, src_ref, out_ref, sem = refs
    out_hbm = out_ref
    i = pl.program_id(0)

    if fill_rows:
        @pl.when(i == 0)
        def _():
            zbuf[...] = jnp.zeros(zbuf.shape, F32)
            for g in range(N_EXPERT_GROUPS):
                @pl.when(fill_ref[N_EXPERT_GROUPS + g] > 0)
                def _():
                    start = pl.multiple_of(fill_ref[g], fill_rows)
                    cp = pltpu.make_async_copy(zbuf, out_hbm.at[pl.ds(start, fill_rows), :], zsem)
                    cp.start()
                    cp.wait()
            n_used, n_tiles = fill_ref[2 * N_EXPERT_GROUPS], fill_ref[2 * N_EXPERT_GROUPS + 1]
            for k in range(N_EXPERT_GROUPS):
                @pl.when(n_used + k < n_tiles)
                def _():
                    start = pl.multiple_of((n_used + k) * fill_rows, fill_rows)
                    cp = pltpu.make_async_copy(zbuf, out_hbm.at[pl.ds(start, fill_rows), :], zsem)
                    cp.start()
                    cp.wait()

    group = 16

    def body(j, carry):
        r0 = pl.multiple_of(j * group, group)
        for k in range(group):
            p = idx_ref[0, 0, r0 + k]
            if scatter:
                cp = pltpu.make_async_copy(src_ref.at[pl.ds(r0 + k, 1), :], out_ref.at[pl.ds(p, 1), :], sem)
            else:
                cp = pltpu.make_async_copy(src_ref.at[pl.ds(p, 1), :], out_ref.at[pl.ds(r0 + k, 1), :], sem)
            cp.start()
        return carry

    lax.fori_loop(0, tn // group, body, 0)
    if scatter:
        pltpu.make_async_copy(src_ref, out_ref.at[pl.ds(0, tn), :], sem).wait()
    else:
        pltpu.make_async_copy(src_ref.at[pl.ds(0, tn), :], out_ref, sem).wait()


def _permute_rows(src, idx, *, n_rows, n_out, scatter, fill=None, fill_rows=0):
    w = src.shape[1]
    tn = _pick_tile(n_rows, 1024)
    steps = n_rows // tn
    idx3 = idx.reshape(steps, 1, tn)
    block = pl.BlockSpec((tn, w), lambda i, *_: (i, 0))
    hbm = pl.BlockSpec(memory_space=pl.ANY)
    in_specs = [pl.BlockSpec((1, 1, tn), lambda i, *_: (i, 0, 0), memory_space=pltpu.SMEM),
                block if scatter else hbm]
    scratch = [pltpu.SemaphoreType.DMA(())]
    args = [idx3, src]
    n_prefetch = 0
    if fill_rows:
        n_prefetch = 1
        args = [fill] + args
        scratch = [pltpu.VMEM((fill_rows, w), F32), pltpu.SemaphoreType.DMA(()), pltpu.SemaphoreType.DMA(())]
    return pl.pallas_call(
        functools.partial(_permute_kernel, tn=tn, scatter=scatter, fill_rows=fill_rows),
        grid_spec=pltpu.PrefetchScalarGridSpec(
            num_scalar_prefetch=n_prefetch, grid=(steps,), in_specs=in_specs,
            out_specs=hbm if scatter else block, scratch_shapes=scratch),
        out_shape=jax.ShapeDtypeStruct((n_out, w), F32),
        compiler_params=_cparams(("arbitrary",)),
        name="permute_scatter" if scatter else "permute_gather",
    )(*args)


def _moe_kernel(grp_ref, nused_ref, xs_ref, g2_ref, wg_ref, wu_ref, wd_ref, fnw_ref, o_ref, *, tm, final, n_mod):
    used = pl.program_id(0) < nused_ref[0]

    @pl.when(jnp.logical_not(used))
    def _():
        o_ref[...] = jnp.zeros(o_ref.shape, F32)

    @pl.when(used)
    def _():
        rows = xs_ref[...]
        h = rows[:, ROW_H:ROW_H + D_MODEL].astype(BF16)
        y = None
        for k in range(EXPERTS_PER_GROUP):
            act = _silu(_dot(h, wg_ref[k])) * _dot(h, wu_ref[k])
            gate = rows[:, ROW_AUX + k:ROW_AUX + k + 1]
            part = _dot((act * gate).astype(BF16), wd_ref[k])
            y = part if y is None else y + part
        if n_mod == 1:
            g2 = g2_ref[0:1, :]
        else:
            mod_row = rows[:, ROW_AUX + AUX_MODROW:ROW_AUX + AUX_MODROW + 1]
            ids = lax.broadcasted_iota(jnp.int32, (tm, g2_ref.shape[0]), 1).astype(F32)
            onehot = jnp.where(mod_row == ids, 1.0, 0.0).astype(BF16)
            g_hi, g_lo = _split2(g2_ref[...])
            g2 = _dot(onehot, g_hi) + _dot(onehot, g_lo)
        xn = rows[:, ROW_X:ROW_X + D_MODEL] + g2 * y
        if final:
            xn = _rms(xn) * fnw_ref[...]
        o_ref[...] = xn


def _moe(rows, gid, g2, lw, final_norm_w, *, n_tokens, final):
    t, d, tm = n_tokens, D_MODEL, MOE_TILE
    dest, tile_grp, n_used, fill = _moe_plan(gid.reshape(t), t, tm)
    nt = tile_grp.shape[0]
    n_sorted = nt * tm
    n_mod = g2.shape[0]
    if n_mod > 1:
        g2 = jnp.pad(g2, ((0, (-n_mod) % 8), (0, 0)))
    eg = EXPERTS_PER_GROUP
    xs = _permute_rows(rows, dest, n_rows=t, n_out=n_sorted, scatter=True, fill=fill, fill_rows=tm)

    def tile(i, grp, nu):
        return jnp.minimum(i, nu[0] - 1)

    grid_spec = pltpu.PrefetchScalarGridSpec(
        num_scalar_prefetch=2,
        grid=(nt,),
        in_specs=[pl.BlockSpec((tm, ROW_W), lambda i, grp, nu: (tile(i, grp, nu), 0)),
                  pl.BlockSpec(g2.shape, lambda i, grp, nu: (0, 0)),
                  pl.BlockSpec((eg, d, D_EXPERT), lambda i, grp, nu: (grp[tile(i, grp, nu)], 0, 0)),
                  pl.BlockSpec((eg, d, D_EXPERT), lambda i, grp, nu: (grp[tile(i, grp, nu)], 0, 0)),
                  pl.BlockSpec((eg, D_EXPERT, d), lambda i, grp, nu: (grp[tile(i, grp, nu)], 0, 0)),
                  pl.BlockSpec((1, d), lambda i, grp, nu: (0, 0))],
        out_specs=pl.BlockSpec((tm, d), lambda i, grp, nu: (i, 0)))
    ys = pl.pallas_call(
        functools.partial(_moe_kernel, tm=tm, final=final, n_mod=n_mod),
        grid_spec=grid_spec,
        out_shape=jax.ShapeDtypeStruct((n_sorted, d), F32),
        compiler_params=_cparams(("arbitrary",)),
        name="moe",
    )(tile_grp, n_used, xs, g2, lw["w_gate_e"], lw["w_up_e"], lw["w_down_e"], final_norm_w)
    return _permute_rows(ys, dest, n_rows=t, n_out=t, scatter=False)


def _grid_pos_embed(n_tokens, dim):
    rows = n_tokens // GRID_W
    quarter = dim // 4
    omega = 1.0 / (10000.0 ** (jnp.arange(quarter, dtype=F32) / quarter))
    r = jnp.arange(rows, dtype=F32)[:, None] * omega
    col = jnp.arange(GRID_W, dtype=F32)[:, None] * omega
    r_emb = jnp.concatenate([jnp.sin(r), jnp.cos(r)], axis=-1)
    c_emb = jnp.concatenate([jnp.sin(col), jnp.cos(col)], axis=-1)
    emb = jnp.concatenate([
        jnp.broadcast_to(r_emb[:, None, :], (rows, GRID_W, dim // 2)),
        jnp.broadcast_to(c_emb[None, :, :], (rows, GRID_W, dim // 2))], axis=-1)
    return emb.reshape(rows * GRID_W, dim)


def _dt_perm():
    return np.array([dr * SSD_HEADS + g * SSD_HPG + j
                     for g in range(SSD_GROUPS) for dr in range(2) for j in range(SSD_HPG)], np.int32)


def _layer_weights(l, w_in, norm1_w, gmlp_norm_w, gmlp_ws, gmlp_bs, conv_w, conv_b, dt_bias, a_log, d_skip,
                   ssd_norm_w, w_branch_a, w_branch_b, w_out, norm2_w, router_w, router_bias,
                   w_gate_e, w_up_e, w_down_e):
    perm = _dt_perm()
    w_dt = jnp.pad(w_in[l][:, MAIN_COLS:][:, perm], ((0, 0), (0, DT_PAD - 2 * SSD_HEADS)))
    prow = jnp.stack([dt_bias[l].reshape(-1)[perm], a_log[l].reshape(-1)[perm]], axis=0)
    prow = prow.reshape(2, SSD_GROUPS, 16).transpose(1, 0, 2)
    return {
        "norm1_w": norm1_w[l][None],
        "w_main": w_in[l][:, :MAIN_COLS].astype(BF16),
        "w_dt": w_dt.astype(BF16),
        "gmlp_norm_w": gmlp_norm_w[l][None],
        "gmlp_ws": gmlp_ws[l].astype(BF16),
        "gmlp_bs_x": jnp.repeat(gmlp_bs[l].T, GMLP_WIDTH // GMLP_GROUPS, axis=1),
        "conv_w": conv_w[l],
        "conv_b": conv_b[l][None],
        "ssd_pcol": prow.transpose(0, 2, 1),
        "d_skip_x": jnp.repeat(d_skip[l], SSD_HEADDIM)[None],
        "ssd_norm_w": ssd_norm_w[l][None],
        "w_branch_a": w_branch_a[l].astype(BF16),
        "w_branch_b": w_branch_b[l].astype(BF16),
        "w_out": w_out[l].astype(BF16),
        "norm2_w": norm2_w[l][None],
        "router_wt": router_w.T,
        "router_bias": router_bias[:, None],
        "w_gate_e": w_gate_e[l].astype(BF16),
        "w_up_e": w_up_e[l].astype(BF16),
        "w_down_e": w_down_e[l].astype(BF16),
    }


def _layer(x, pos, mod, lw, h0, final_norm_w, *, batch, seq_len, layer, emit_state, final):
    t = batch * seq_len
    proj, dtr = _inproj(x, pos, mod, lw["norm1_w"], lw["w_main"], lw["w_dt"], n_tokens=t)
    yb, state = _ssd(proj, dtr, lw, h0, batch=batch, seq_len=seq_len, layer=layer, emit_state=emit_state)
    rows, gid = _mixout(x, pos, proj, yb, mod, lw, n_tokens=t)
    out = _moe(rows, gid, mod[:, N_MOD - 1, :], lw, final_norm_w, n_tokens=t, final=final)
    return out, state


def kernel(x_prompt, x_sample, state_ssd, c, c_ctx, w_mod, b_mod, norm1_w, w_in, gmlp_norm_w, gmlp_ws, gmlp_bs, conv_w, conv_b, dt_bias, a_log, d_skip, ssd_norm_w, w_branch_a, w_branch_b, w_out, norm2_w, router_w, router_bias, w_gate_e, w_up_e, w_down_e, final_norm_w):
    bp, lp, d = x_prompt.shape
    bs, ls, _ = x_sample.shape
    depth = w_mod.shape[0]
    assert 1 + bs <= MOD_ROWS and d == D_MODEL
    assert lp % (2 * CHUNK) == 0 and ls % (2 * CHUNK) == 0

    cond = jnp.concatenate([c_ctx[None], c, jnp.zeros((MOD_ROWS - 1 - bs, d), F32)], axis=0)
    mod = _modulation(cond, w_mod, b_mod).reshape(depth, MOD_ROWS, N_MOD, d)

    pos = _grid_pos_embed(ls, d)
    xp = x_prompt.reshape(bp * lp, d)
    xs = x_sample.reshape(bs * ls, d)
    cache = state_ssd.reshape(bs, depth, 2, SSD_GROUPS, GROUP_CH, SSD_STATE)
    fnw = final_norm_w[None]

    states = []
    for l in range(depth):
        lw = _layer_weights(l, w_in, norm1_w, gmlp_norm_w, gmlp_ws, gmlp_bs, conv_w, conv_b, dt_bias, a_log,
                            d_skip, ssd_norm_w, w_branch_a, w_branch_b, w_out, norm2_w, router_w, router_bias,
                            w_gate_e, w_up_e, w_down_e)
        final = l == depth - 1
        xp, st = _layer(xp, None, mod[l, 0:1], lw, None, fnw, batch=bp, seq_len=lp, layer=l,
                        emit_state=True, final=final)
        states.append(st)
        xs, _ = _layer(xs, pos if l == 0 else None, mod[l, 1:1 + bs], lw, cache, fnw, batch=bs, seq_len=ls,
                       layer=l, emit_state=False, final=final)

    new_state = jnp.stack(states, axis=1).reshape(bp, depth, 2, SSD_HEADS, SSD_HEADDIM, SSD_STATE)
    return (xp.reshape(bp, lp, d), xs.reshape(bs, ls, d), new_state.astype(x_prompt.dtype))
```

```python
import functools
import math

import numpy as np
import jax
import jax.numpy as jnp
from jax import lax
from jax.experimental import pallas as pl
from jax.experimental.pallas import tpu as pltpu

F32 = jnp.float32
BF16 = jnp.bfloat16

D_MODEL = 1024
CHUNK = 128
GRID_W = 64
GMLP_WIDTH = 1024
GMLP_GROUPS = 8
SSD_INNER = 2048
SSD_HEADDIM = 64
SSD_HEADS = 32
SSD_GROUPS = 4
SSD_HPG = 8
SSD_STATE = 128
SSD_CONV = 5
GROUP_CH = SSD_INNER // SSD_GROUPS
N_EXPERTS = 16
N_EXPERT_GROUPS = 4
EXPERTS_PER_GROUP = 4
D_EXPERT = 512
N_MOD = 6
EPS = 1e-6
MAIN_COLS = 2 * D_MODEL + 2 * GMLP_WIDTH + SSD_INNER + SSD_INNER + 2 * SSD_GROUPS * SSD_STATE
COL_Z = 2 * D_MODEL + 2 * GMLP_WIDTH
COL_X = COL_Z + SSD_INNER
COL_B = COL_X + SSD_INNER
COL_C = COL_B + SSD_GROUPS * SSD_STATE
DT_PAD = 128
ROW_X = 0
ROW_H = D_MODEL
ROW_AUX = 2 * D_MODEL
PAIRS_PER_GROUP = 6
N_BUCKETS = 24
ROW_W = 2 * D_MODEL + 128
AUX_MODROW = 4
MOE_TILE = 256
MIX_SUB = 256
MOD_ROWS = 16

V7X_VMEM_LIMIT = 56 * 1024 * 1024


def _cparams(sem):
    return pltpu.CompilerParams(dimension_semantics=sem, vmem_limit_bytes=V7X_VMEM_LIMIT)


def _pick_tile(rows, preferred):
    tm = preferred
    while rows % tm:
        tm //= 2
    assert tm >= CHUNK
    return tm


def _split3(a):
    hi = a.astype(BF16)
    r1 = a - hi.astype(F32)
    mid = r1.astype(BF16)
    lo = (r1 - mid.astype(F32)).astype(BF16)
    return hi, mid, lo


def _split2(a):
    hi = a.astype(BF16)
    lo = (a - hi.astype(F32)).astype(BF16)
    return hi, lo


def _dot(a, b):
    return jnp.dot(a, b, preferred_element_type=F32)


def _dot_nt(a, b):
    return lax.dot_general(a, b, (((1,), (1,)), ((), ())), preferred_element_type=F32)


def _silu(x):
    return x * jax.nn.sigmoid(x)


def _gelu_tanh(x):
    c = math.sqrt(2.0 / math.pi)
    return x * (0.5 * (1.0 + jnp.tanh(c * (x + 0.044715 * (x * x * x)))))


def _softplus(x):
    return jnp.maximum(x, 0.0) + jnp.log1p(jnp.exp(-jnp.abs(x)))


def _rms(x):
    return x * lax.rsqrt(jnp.mean(x * x, axis=-1, keepdims=True) + EPS)


def _mod_kernel(cond_ref, w_ref, b_ref, o_ref):
    c = cond_ref[...]
    s_hi, s_lo = _split2(_silu(c))
    w_hi, w_lo = _split2(w_ref[0])
    o_ref[0] = _dot(s_hi, w_hi) + _dot(s_hi, w_lo) + _dot(s_lo, w_hi) + b_ref[0]


def _modulation(cond, w_mod, b_mod):
    depth, d, n = w_mod.shape
    tn = 1536
    return pl.pallas_call(
        _mod_kernel,
        grid=(depth, n // tn),
        in_specs=[pl.BlockSpec((MOD_ROWS, d), lambda l, j: (0, 0)),
                  pl.BlockSpec((1, d, tn), lambda l, j: (l, 0, j)),
                  pl.BlockSpec((1, 1, tn), lambda l, j: (l, 0, j))],
        out_specs=pl.BlockSpec((1, MOD_ROWS, tn), lambda l, j: (l, 0, j)),
        out_shape=jax.ShapeDtypeStruct((depth, MOD_ROWS, n), F32),
        compiler_params=_cparams(("arbitrary", "arbitrary")),
        name="modulation",
    )(cond, w_mod, b_mod.reshape(depth, 1, n))


def _inproj_kernel(*refs, has_pos, tm):
    if has_pos:
        x_ref, pos_ref, mod_ref, nw_ref, w_ref, wdt_ref, proj_ref, dtr_ref, h_scr = refs
    else:
        x_ref, mod_ref, nw_ref, w_ref, wdt_ref, proj_ref, dtr_ref, h_scr = refs
        pos_ref = None

    @pl.when(pl.program_id(1) == 0)
    def _():
        x = x_ref[...]
        if has_pos:
            x = x + pos_ref[...]
        h = (_rms(x) * nw_ref[...]) * (1.0 + mod_ref[0, 1:2, :]) + mod_ref[0, 0:1, :]
        hb = h.astype(BF16)
        h_scr[...] = hb
        dt = _dot(hb, wdt_ref[...])
        dtt = dt.T
        for g in range(SSD_GROUPS):
            dtr_ref[g] = dtt[g * 16:(g + 1) * 16, :]

    proj_ref[...] = _dot(h_scr[...], w_ref[...]).astype(BF16)


def _inproj(x, pos, mod, norm_w, w_main, w_dt, *, n_tokens):
    t, d = n_tokens, x.shape[1]
    tn = 1536
    nb = mod.shape[0]
    rows_per_mod = t // nb
    tm = _pick_tile(rows_per_mod, 1024)
    has_pos = pos is not None
    in_specs = [pl.BlockSpec((tm, d), lambda i, j: (i, 0))]
    args = [x]
    if has_pos:
        pos_blocks = pos.shape[0] // tm
        in_specs.append(pl.BlockSpec((tm, d), lambda i, j: (i % pos_blocks, 0)))
        args.append(pos)
    in_specs += [pl.BlockSpec((1, N_MOD, d), lambda i, j: ((i * tm) // rows_per_mod, 0, 0)),
                 pl.BlockSpec((1, d), lambda i, j: (0, 0)),
                 pl.BlockSpec((d, tn), lambda i, j: (0, j)),
                 pl.BlockSpec((d, DT_PAD), lambda i, j: (0, 0))]
    args += [mod, norm_w, w_main, w_dt]
    return pl.pallas_call(
        functools.partial(_inproj_kernel, has_pos=has_pos, tm=tm),
        grid=(t // tm, MAIN_COLS // tn),
        in_specs=in_specs,
        out_specs=[pl.BlockSpec((tm, tn), lambda i, j: (i, j)),
                   pl.BlockSpec((SSD_GROUPS, 16, tm), lambda i, j: (0, 0, i))],
        out_shape=[jax.ShapeDtypeStruct((t, MAIN_COLS), BF16),
                   jax.ShapeDtypeStruct((SSD_GROUPS, 16, t), F32)],
        scratch_shapes=[pltpu.VMEM((tm, d), BF16)],
        compiler_params=_cparams(("arbitrary", "arbitrary")),
        name="inproj",
    )(*args)


def _tri_dot_right(a, tri):
    hi, mid, lo = _split3(a)
    return _dot(hi, tri) + _dot(mid, tri) + _dot(lo, tri)


COL_ACS = 0
COL_E = 16
COL_W = 32
ROW_Q = 0
ROW_LD = 16


def _ssd_kernel(*refs, seq_len, has_h0, emit_state, n_prev):
    it = iter(refs)
    x_ref, b_ref, c_ref, z_ref, dtr_ref = (next(it) for _ in range(5))
    cwx_ref, cwb_ref, cwc_ref, cbx_ref, cbb_ref, cbc_ref = (next(it) for _ in range(6))
    pcol_ref, dsk_ref, nw_ref, fanf_ref, fanb_ref = (next(it) for _ in range(5))
    h0_ref = next(it) if has_h0 else None
    prev_ref = next(it) if n_prev else None
    y_ref = next(it)
    st_ref = next(it) if emit_state else None
    (pad_scr, xs_scr, cbf_scr, bbf_scr, btf_scr, yb_scr, yf_scr, dall_scr, acs_scr, e_scr, w_scr, q_scr, ld_scr,
     rt_scr, col_scr, rq_scr, hf_scr, hb_scr, xbd_scr, yd_scr) = (next(it) for _ in range(20))

    L = seq_len
    nc = L // CHUNK
    W = GROUP_CH + 2 * SSD_STATE
    n_xblk = GROUP_CH // 128

    zeros8 = jnp.zeros((8, W), F32)
    pad_scr[0:8, :] = zeros8
    pad_scr[L + 8:L + 16, :] = zeros8

    def stage(c, carry):
        r0 = pl.multiple_of(c * CHUNK, CHUNK)
        pad_scr[pl.ds(r0 + 8, CHUNK), 0:GROUP_CH] = x_ref[0, pl.ds(r0, CHUNK), :].astype(F32)
        pad_scr[pl.ds(r0 + 8, CHUNK), GROUP_CH:GROUP_CH + SSD_STATE] = b_ref[0, pl.ds(r0, CHUNK), :].astype(F32)
        pad_scr[pl.ds(r0 + 8, CHUNK), GROUP_CH + SSD_STATE:W] = c_ref[0, pl.ds(r0, CHUNK), :].astype(F32)
        return carry

    lax.fori_loop(0, nc, stage, 0)

    cw = [cwx_ref[...][:, k * 128:(k + 1) * 128] for k in range(GROUP_CH // 128)] + [cwb_ref[...], cwc_ref[...]]
    cb = [cbx_ref[...][:, k * 128:(k + 1) * 128] for k in range(GROUP_CH // 128)] + [cbb_ref[...], cbc_ref[...]]
    win_rows = CHUNK + 16

    def conv(c, carry):
        r0 = pl.multiple_of(c * CHUNK, CHUNK)
        for k in range(W // 128):
            win = pad_scr[pl.ds(r0, win_rows), k * 128:(k + 1) * 128]
            acc = win[8:8 + CHUNK, :] * cw[k][2:3, :]
            for tap in (0, 1, 3, 4):
                d = tap - 2
                rolled = pltpu.roll(win, (win_rows - d) % win_rows, 0)
                acc = acc + rolled[8:8 + CHUNK, :] * cw[k][tap:tap + 1, :]
            acc = _silu(acc + cb[k])
            if k < n_xblk:
                xs_scr[pl.ds(r0, CHUNK), k * 128:(k + 1) * 128] = acc
            elif k == n_xblk:
                bbf_scr[pl.ds(r0, CHUNK), :] = acc.astype(BF16)
                btf_scr[c] = acc.T.astype(BF16)
            else:
                cbf_scr[pl.ds(r0, CHUNK), :] = acc.astype(BF16)
        return carry

    lax.fori_loop(0, nc, conv, 0)

    ri = lax.broadcasted_iota(jnp.int32, (CHUNK, CHUNK), 0)
    ci = lax.broadcasted_iota(jnp.int32, (CHUNK, CHUNK), 1)
    below = ci < ri
    above = ci > ri
    tri_ge = jnp.where(ci >= ri, 1.0, 0.0).astype(BF16)
    tri_le = jnp.where(ci <= ri, 1.0, 0.0).astype(BF16)
    left_half = ci < SSD_HEADDIM

    bias_col = pcol_ref[0, :, 0:1]
    aneg_col = -jnp.exp(pcol_ref[0, :, 1:2])
    for c in range(nc):
        dall_scr[c * 16:(c + 1) * 16, :] = _softplus(dtr_ref[0, :, c * CHUNK:(c + 1) * CHUNK] + bias_col)
    d_all = dall_scr[...]
    nr = nc * 16
    a_all = d_all * jnp.concatenate([aneg_col] * nc, axis=0)
    fwd_rows = (lax.broadcasted_iota(jnp.int32, (nr, 1), 0) & SSD_HPG) == 0
    acs = jnp.where(fwd_rows, _tri_dot_right(a_all, tri_ge), _tri_dot_right(a_all, tri_le))
    total = jnp.where(fwd_rows, acs[:, CHUNK - 1:CHUNK], acs[:, 0:1])
    acs_scr[...] = acs
    e_scr[...] = jnp.exp(acs)
    w_scr[...] = d_all * jnp.exp(total - acs)
    q_scr[...] = acs - jnp.log(d_all)
    ld_scr[...] = jnp.log(d_all + pltpu.roll(d_all, nr - SSD_HPG, 0))

    rt_scr[...] = jnp.zeros(rt_scr.shape, F32)
    for c in range(nc):
        rows = slice(c * 16, (c + 1) * 16)
        rt_scr[COL_ACS:COL_ACS + 16, :] = acs_scr[rows, :]
        rt_scr[COL_E:COL_E + 16, :] = e_scr[rows, :]
        rt_scr[COL_W:COL_W + 16, :] = w_scr[rows, :]
        col_scr[c] = rt_scr[...].T
        rq_scr[c, ROW_Q:ROW_Q + 16, :] = q_scr[rows, :]
        rq_scr[c, ROW_LD:ROW_LD + SSD_HPG, :] = ld_scr[c * 16:c * 16 + SSD_HPG, :]

    def fans(col, fan_ref):
        hi, lo = _split2(col)
        both = _dot(jnp.concatenate([hi, lo], axis=1), fan_ref[...])
        return both[:, 0:GROUP_CH], both[:, GROUP_CH:2 * GROUP_CH]

    if has_h0:
        hf_scr[...] = h0_ref[0, 0, 0, 0].T
        hb_scr[...] = h0_ref[0, 0, 1, 0].T
    else:
        hf_scr[...] = jnp.zeros((SSD_STATE, GROUP_CH), F32)
        hb_scr[...] = jnp.zeros((SSD_STATE, GROUP_CH), F32)
    xbd_scr[...] = jnp.zeros(xbd_scr.shape, BF16)

    n_half = GROUP_CH // 256

    def sweeps(i, carry):
        passes = ((nc - 1 - i, hb_scr, fanb_ref, yb_scr, 0), (i, hf_scr, fanf_ref, yf_scr, CHUNK - 1))
        staged = []
        for c, h_scr, fan_ref, y_scr, total_row in passes:
            r0 = pl.multiple_of(c * CHUNK, CHUNK)
            e_x, w_x = fans(col_scr[c], fan_ref)
            staged.append((r0, cbf_scr[pl.ds(r0, CHUNK), :], btf_scr[c], e_x, w_x))
        for hf in range(n_half):
            sl = slice(hf * 256, (hf + 1) * 256)
            for (c, h_scr, fan_ref, y_scr, total_row), (r0, cm, bt, e_x, w_x) in zip(passes, staged):
                e2 = e_x[:, sl]
                h = h_scr[:, sl]
                y_scr[pl.ds(r0, CHUNK), sl] = _dot(cm, h.astype(BF16)) * e2
                xw = (xs_scr[pl.ds(r0, CHUNK), sl] * w_x[:, sl]).astype(BF16)
                h_scr[:, sl] = h * e2[total_row:total_row + 1, :] + _dot(bt, xw)
        return carry

    lax.fori_loop(0, nc, sweeps, 0)

    def finish(j, carry):
        work = []
        for u in range(2):
            c = 2 * j + u
            r0 = pl.multiple_of(c * CHUNK, CHUNK)
            cb_mat = _dot_nt(cbf_scr[pl.ds(r0, CHUNK), :], bbf_scr[pl.ds(r0, CHUNK), :])
            work.append((u, r0, col_scr[c], rq_scr[c], cb_mat))

        for u, r0, col, rq, cb_mat in work:
            for p in range(n_xblk):
                xb = xs_scr[pl.ds(r0, CHUNK), p * 128:(p + 1) * 128].astype(BF16)
                xbd_scr[u, p, 0:CHUNK, 0:SSD_HEADDIM] = xb[:, 0:SSD_HEADDIM]
                xbd_scr[u, p, CHUNK:2 * CHUNK, SSD_HEADDIM:2 * SSD_HEADDIM] = xb[:, SSD_HEADDIM:2 * SSD_HEADDIM]
        for p in range(n_xblk):
            for u, r0, col, rq, cb_mat in work:
                wpair = []
                for hh in (2 * p, 2 * p + 1):
                    pf = col[:, COL_ACS + hh:COL_ACS + hh + 1]
                    pb = col[:, COL_ACS + SSD_HPG + hh:COL_ACS + SSD_HPG + hh + 1]
                    qf = rq[ROW_Q + hh:ROW_Q + hh + 1, :]
                    qb = rq[ROW_Q + SSD_HPG + hh:ROW_Q + SSD_HPG + hh + 1, :]
                    ld = rq[ROW_LD + hh:ROW_LD + hh + 1, :]
                    arg = jnp.where(below, pf - qf, jnp.where(above, pb - qb, ld))
                    wpair.append((cb_mat * jnp.exp(arg)).astype(BF16))
                yd_scr[u, :, p * 128:(p + 1) * 128] = _dot(jnp.concatenate(wpair, axis=1), xbd_scr[u, p])

        for u, r0, col, rq, cb_mat in work:
            ssq = jnp.zeros((CHUNK, 1), F32)
            for hf in range(n_half):
                sl = slice(hf * 256, (hf + 1) * 256)
                y = (yb_scr[pl.ds(r0, CHUNK), sl] + yf_scr[pl.ds(r0, CHUNK), sl] + yd_scr[u, :, sl]
                     + xs_scr[pl.ds(r0, CHUNK), sl] * dsk_ref[:, sl])
                yg = y * _silu(z_ref[0, pl.ds(r0, CHUNK), sl].astype(F32))
                yd_scr[u, :, sl] = yg
                ssq = ssq + jnp.sum(yg * yg, axis=1, keepdims=True)
            scale = lax.rsqrt(ssq * (1.0 / GROUP_CH) + EPS)
            for hf in range(n_half):
                sl = slice(hf * 256, (hf + 1) * 256)
                y_ref[0, pl.ds(r0, CHUNK), sl] = (yd_scr[u, :, sl] * scale * nw_ref[:, sl]).astype(BF16)
        return carry

    lax.fori_loop(0, nc // 2, finish, 0)

    if emit_state:
        for l in range(n_prev):
            st_ref[0, l] = prev_ref[0, l]
        st_ref[0, n_prev, 0, 0] = hf_scr[...].T
        st_ref[0, n_prev, 1, 0] = hb_scr[...].T


def _fan_matrix(first_head):
    m = np.zeros((2 * CHUNK, 2 * GROUP_CH), np.float32)
    for piece in range(2):
        for part, base in enumerate((COL_E, COL_W)):
            for j in range(SSD_HPG):
                m[piece * CHUNK + base + first_head + j,
                  part * GROUP_CH + j * SSD_HEADDIM:part * GROUP_CH + (j + 1) * SSD_HEADDIM] = 1.0
    return jnp.asarray(m, BF16)


def _ssd(proj, dtr, lw, h0, prev_state, *, batch, seq_len, layer, emit_state):
    L = seq_len
    proj3 = proj.reshape(batch, L, MAIN_COLS)
    has_h0 = h0 is not None
    gc = GROUP_CH
    W = gc + 2 * SSD_STATE

    in_specs = [
        pl.BlockSpec((1, L, gc), lambda b, g: (b, 0, COL_X // gc + g)),
        pl.BlockSpec((1, L, SSD_STATE), lambda b, g: (b, 0, COL_B // SSD_STATE + g)),
        pl.BlockSpec((1, L, SSD_STATE), lambda b, g: (b, 0, COL_C // SSD_STATE + g)),
        pl.BlockSpec((1, L, gc), lambda b, g: (b, 0, COL_Z // gc + g)),
        pl.BlockSpec((1, 16, L), lambda b, g: (g, 0, b)),
        pl.BlockSpec((SSD_CONV, gc), lambda b, g: (0, g)),
        pl.BlockSpec((SSD_CONV, SSD_STATE), lambda b, g: (0, SSD_INNER // SSD_STATE + g)),
        pl.BlockSpec((SSD_CONV, SSD_STATE), lambda b, g: (0, SSD_INNER // SSD_STATE + SSD_GROUPS + g)),
        pl.BlockSpec((1, gc), lambda b, g: (0, g)),
        pl.BlockSpec((1, SSD_STATE), lambda b, g: (0, SSD_INNER // SSD_STATE + g)),
        pl.BlockSpec((1, SSD_STATE), lambda b, g: (0, SSD_INNER // SSD_STATE + SSD_GROUPS + g)),
        pl.BlockSpec((1, 16, 2), lambda b, g: (g, 0, 0)),
        pl.BlockSpec((1, gc), lambda b, g: (0, g)),
        pl.BlockSpec((1, gc), lambda b, g: (0, g)),
        pl.BlockSpec((2 * CHUNK, 2 * gc), lambda b, g: (0, 0)),
        pl.BlockSpec((2 * CHUNK, 2 * gc), lambda b, g: (0, 0)),
    ]
    args = [proj3, proj3, proj3, proj3, dtr,
            lw["conv_w"], lw["conv_w"], lw["conv_w"], lw["conv_b"], lw["conv_b"], lw["conv_b"],
            lw["ssd_pcol"], lw["d_skip_x"], lw["ssd_norm_w"], _fan_matrix(0), _fan_matrix(SSD_HPG)]
    if has_h0:
        in_specs.append(pl.BlockSpec((1, 1, 2, 1, gc, SSD_STATE), lambda b, g: (b, layer, 0, g, 0, 0)))
        args.append(h0)
    out_specs = [pl.BlockSpec((1, L, gc), lambda b, g: (b, 0, g))]
    out_shape = [jax.ShapeDtypeStruct((batch, L, SSD_INNER), BF16)]
    n_prev = 0 if prev_state is None else prev_state.shape[1]
    if n_prev:
        in_specs.append(pl.BlockSpec((1, n_prev, 2, 1, gc, SSD_STATE), lambda b, g: (b, 0, 0, g, 0, 0)))
        args.append(prev_state)
    if emit_state:
        out_specs.append(pl.BlockSpec((1, n_prev + 1, 2, 1, gc, SSD_STATE), lambda b, g: (b, 0, 0, g, 0, 0)))
        out_shape.append(jax.ShapeDtypeStruct((batch, n_prev + 1, 2, SSD_GROUPS, gc, SSD_STATE), F32))
    nc = L // CHUNK
    scratch = [
        pltpu.VMEM((L + 16, W), F32),
        pltpu.VMEM((L, gc), F32),
        pltpu.VMEM((L, SSD_STATE), BF16),
        pltpu.VMEM((L, SSD_STATE), BF16),
        pltpu.VMEM((nc, SSD_STATE, CHUNK), BF16),
        pltpu.VMEM((L, gc), F32),
        pltpu.VMEM((L, gc), F32),
        pltpu.VMEM((nc * 16, CHUNK), F32),
        pltpu.VMEM((nc * 16, CHUNK), F32),
        pltpu.VMEM((nc * 16, CHUNK), F32),
        pltpu.VMEM((nc * 16, CHUNK), F32),
        pltpu.VMEM((nc * 16, CHUNK), F32),
        pltpu.VMEM((nc * 16, CHUNK), F32),
        pltpu.VMEM((CHUNK, CHUNK), F32),
        pltpu.VMEM((nc, CHUNK, CHUNK), F32),
        pltpu.VMEM((nc, 24, CHUNK), F32),
        pltpu.VMEM((SSD_STATE, gc), F32),
        pltpu.VMEM((SSD_STATE, gc), F32),
        pltpu.VMEM((2, gc // 128, 2 * CHUNK, 128), BF16),
        pltpu.VMEM((2, CHUNK, gc), F32),
    ]
    outs = pl.pallas_call(
        functools.partial(_ssd_kernel, seq_len=L, has_h0=has_h0, emit_state=emit_state, n_prev=n_prev),
        grid=(batch, SSD_GROUPS),
        in_specs=in_specs,
        out_specs=out_specs,
        out_shape=out_shape,
        scratch_shapes=scratch,
        compiler_params=_cparams(("arbitrary", "arbitrary")),
        name="ssd",
    )(*args)
    y = outs[0].reshape(batch * L, SSD_INNER)
    return y, (outs[1] if emit_state else None)


def _route(sel, scores):
    neg = -jnp.inf

    def first_max(vals):
        m = vals[0]
        for v in vals[1:]:
            m = jnp.maximum(m, v)
        taken = jnp.zeros_like(m)
        flags = []
        for v in vals:
            f = jnp.where(v == m, 1.0, 0.0) * (1.0 - taken)
            flags.append(f)
            taken = taken + f
        return m, flags

    group_scores = []
    for j in range(N_EXPERT_GROUPS):
        a = sel[j * EXPERTS_PER_GROUP:(j + 1) * EXPERTS_PER_GROUP]
        m1, f1 = first_max(a)
        m2, _ = first_max([jnp.where(f > 0.5, neg, v) for f, v in zip(f1, a)])
        group_scores.append(m1 + m2)
    _, gflag = first_max(group_scores)
    masked = [jnp.where(gflag[e // EXPERTS_PER_GROUP] > 0.5, sel[e], neg) for e in range(N_EXPERTS)]
    _, f1 = first_max(masked)
    _, f2 = first_max([jnp.where(f > 0.5, neg, v) for f, v in zip(f1, masked)])
    w1 = sum(f * s for f, s in zip(f1, scores))
    w2 = sum(f * s for f, s in zip(f2, scores))
    tot = w1 + w2
    gates = [(f1[e] * w1 + f2[e] * w2) / tot for e in range(N_EXPERTS)]
    taken = jnp.zeros_like(tot)
    e_lo = e_hi = g_lo = g_hi = jnp.zeros_like(tot)
    for e in range(N_EXPERTS):
        sel_e = f1[e] + f2[e]
        low = sel_e * (1.0 - taken)
        high = sel_e - low
        taken = taken + sel_e
        e_lo = e_lo + float(e) * low
        e_hi = e_hi + float(e) * high
        g_lo = g_lo + low * gates[e]
        g_hi = g_hi + high * gates[e]
    group = sum(float(j) * gflag[j] for j in range(1, N_EXPERT_GROUPS))
    a = e_lo - EXPERTS_PER_GROUP * group
    b = e_hi - EXPERTS_PER_GROUP * group
    pair = a * (2 * EXPERTS_PER_GROUP - 1 - a) * 0.5 + (b - a - 1.0)
    bucket = group * PAIRS_PER_GROUP + pair
    return (g_lo, g_hi), bucket


def _mixout_kernel(*refs, has_pos, tm, rows_per_mod):
    it = iter(refs)
    x_ref = next(it)
    pos_ref = next(it) if has_pos else None
    ga_ref, gb_ref, u_ref, v_ref, yb_ref, mod_ref = (next(it) for _ in range(6))
    gnw_ref, ws_ref, bsx_ref, wa_ref, wb_ref, wo_ref, n2w_ref, rwt_ref, rb_ref = (next(it) for _ in range(9))
    rows_ref, gid_ref = (next(it) for _ in range(2))
    g_scr = next(it)

    sub = min(tm, MIX_SUB)
    spans = [slice(s * sub, (s + 1) * sub) for s in range(tm // sub)]
    n_chunks = tm // CHUNK
    gd = GMLP_WIDTH // GMLP_GROUPS

    u = [_gelu_tanh(u_ref[r, :].astype(F32)) for r in spans]
    vn = [(_rms(_gelu_tanh(v_ref[r, :].astype(F32))) * gnw_ref[...]).astype(BF16) for r in spans]

    def chunk_rows(c):
        s, local = divmod(c * CHUNK, sub)
        return s, slice(local, local + CHUNK)

    mixed_cols = [[None] * GMLP_GROUPS for _ in range(n_chunks)]
    for g in range(GMLP_GROUPS):
        pieces = []
        for c in range(n_chunks):
            s, rr = chunk_rows(c)
            pieces.append(vn[s][rr, g * gd:(g + 1) * gd])
        res = _dot(ws_ref[g], jnp.concatenate(pieces, axis=1))
        for c in range(n_chunks):
            mixed_cols[c][g] = res[:, c * gd:(c + 1) * gd]
    per_sub = sub // CHUNK
    ya = []
    for s in range(len(spans)):
        mixed = jnp.concatenate([jnp.concatenate(mixed_cols[s * per_sub + c], axis=1) + bsx_ref[...]
                                 for c in range(per_sub)], axis=0)
        ya.append((u[s] * mixed).astype(BF16))

    da = [_dot(y, wa_ref[...]) for y in ya]
    db = [_dot(yb_ref[r, :], wb_ref[...]) for r in spans]
    merged = [(jax.nn.sigmoid(ga_ref[r, :].astype(F32)) * a + jax.nn.sigmoid(gb_ref[r, :].astype(F32)) * b
               ).astype(BF16) for r, a, b in zip(spans, da, db)]
    do = [_dot(m, wo_ref[...]) for m in merged]

    h2s = []
    for r, o in zip(spans, do):
        x = x_ref[r, :]
        if has_pos:
            x = x + pos_ref[r, :]
        xn = x + mod_ref[0, 2:3, :] * o
        rows_ref[r, ROW_X:ROW_X + D_MODEL] = xn
        h2 = (_rms(xn) * n2w_ref[...]) * (1.0 + mod_ref[0, 4:5, :]) + mod_ref[0, 3:4, :]
        rows_ref[r, ROW_H:ROW_H + D_MODEL] = h2
        h2s.append(h2)

    r_hi, r_lo = _split2(rwt_ref[...])
    g_scr[...] = jnp.zeros(g_scr.shape, F32)
    mod_row = (pl.program_id(0) * tm) // rows_per_mod
    g_scr[AUX_MODROW:AUX_MODROW + 1, :] = jnp.full((1, tm), mod_row, jnp.int32).astype(F32)
    for r, h2 in zip(spans, h2s):
        h_hi, h_lo = _split2(h2)
        logits = _dot_nt(r_hi, h_hi) + _dot_nt(r_hi, h_lo) + _dot_nt(r_lo, h_hi)
        scores = jax.nn.sigmoid(logits)
        selm = scores + rb_ref[...]
        pair_gates, bucket = _route([selm[e:e + 1, :] for e in range(N_EXPERTS)],
                                    [scores[e:e + 1, :] for e in range(N_EXPERTS)])
        gid_ref[:, r] = bucket.astype(jnp.int32)
        for k in range(2):
            g_scr[k:k + 1, r] = pair_gates[k]
    rows_ref[:, ROW_AUX:ROW_W] = g_scr[...].T


def _mixout(x, pos, proj, yb, mod, lw, *, n_tokens):
    t, d = n_tokens, x.shape[1]
    nb = mod.shape[0]
    rows_per_mod = t // nb
    tm = _pick_tile(rows_per_mod, 2 * MIX_SUB)
    has_pos = pos is not None
    full = lambda shape: pl.BlockSpec(shape, lambda i: (0,) * len(shape), pipeline_mode=pl.Buffered(1))
    in_specs = [pl.BlockSpec((tm, d), lambda i: (i, 0))]
    args = [x]
    if has_pos:
        pos_blocks = pos.shape[0] // tm
        in_specs.append(pl.BlockSpec((tm, d), lambda i: (i % pos_blocks, 0)))
        args.append(pos)
    in_specs += [pl.BlockSpec((tm, d), lambda i: (i, 0)),
                 pl.BlockSpec((tm, d), lambda i: (i, 1)),
                 pl.BlockSpec((tm, d), lambda i: (i, 2)),
                 pl.BlockSpec((tm, d), lambda i: (i, 3)),
                 pl.BlockSpec((tm, SSD_INNER), lambda i: (i, 0)),
                 pl.BlockSpec((1, N_MOD, d), lambda i: ((i * tm) // rows_per_mod, 0, 0)),
                 full((1, GMLP_WIDTH)),
                 full((GMLP_GROUPS, CHUNK, CHUNK)),
                 full((CHUNK, GMLP_WIDTH)),
                 full((GMLP_WIDTH, d)),
                 full((SSD_INNER, d)),
                 full((d, d)),
                 full((1, d)),
                 full((N_EXPERTS, d)),
                 full((N_EXPERTS, 1))]
    args += [proj, proj, proj, proj, yb, mod,
             lw["gmlp_norm_w"], lw["gmlp_ws"], lw["gmlp_bs_x"], lw["w_branch_a"], lw["w_branch_b"], lw["w_out"],
             lw["norm2_w"], lw["router_wt"], lw["router_bias"]]
    return pl.pallas_call(
        functools.partial(_mixout_kernel, has_pos=has_pos, tm=tm, rows_per_mod=rows_per_mod),
        grid=(t // tm,),
        in_specs=in_specs,
        out_specs=[pl.BlockSpec((tm, ROW_W), lambda i: (i, 0)),
                   pl.BlockSpec((1, tm), lambda i: (0, i))],
        out_shape=[jax.ShapeDtypeStruct((t, ROW_W), F32),
                   jax.ShapeDtypeStruct((1, t), jnp.int32)],
        scratch_shapes=[pltpu.VMEM((ROW_W - ROW_AUX, tm), F32)],
        compiler_params=_cparams(("arbitrary",)),
        name="mixout",
    )(*args)


def _moe_plan(gid, n_tokens, tm):
    nb = N_BUCKETS
    t = n_tokens
    i32 = jnp.int32
    onehot = (gid[:, None] == jnp.arange(nb, dtype=i32)[None, :]).astype(i32)
    csum = jnp.cumsum(onehot, axis=0)
    rank = jnp.sum(onehot * (csum - 1), axis=1)
    counts = csum[-1]
    padded = ((counts + tm - 1) // tm) * tm
    ends = jnp.cumsum(padded)
    dest = jnp.sum(onehot * (ends - padded)[None, :], axis=1) + rank
    nt = pl.cdiv(t, tm) + nb
    tile_bucket = jnp.sum((jnp.arange(nt, dtype=i32) * tm)[:, None] >= ends[None, :], axis=1)
    tile_bucket = jnp.minimum(tile_bucket, nb - 1)
    pairs = [(a, b) for a in range(EXPERTS_PER_GROUP) for b in range(a + 1, EXPERTS_PER_GROUP)]
    lo = jnp.asarray([g * EXPERTS_PER_GROUP + a for g in range(N_EXPERT_GROUPS) for a, _ in pairs], i32)
    hi = jnp.asarray([g * EXPERTS_PER_GROUP + b for g in range(N_EXPERT_GROUPS) for _, b in pairs], i32)
    n_used = (ends[-1] // tm).astype(i32)
    tail = n_used + jnp.arange(nb, dtype=i32)
    cand = jnp.concatenate([jnp.where(padded > 0, ends - tm, -1), jnp.where(tail < nt, tail * tm, -1)])
    keep = cand >= 0
    order = jnp.argsort(jnp.logical_not(keep), stable=True)
    fill = jnp.concatenate([jnp.sum(keep).reshape(1), cand[order]]).astype(i32)
    return dest.astype(i32), lo[tile_bucket], hi[tile_bucket], n_used.reshape(1), fill


def _permute_kernel(*refs, tn, scatter, fill_rows):
    if fill_rows:
        fill_ref, idx_ref, src_ref, out_ref, zbuf, sem, zsem = refs
    else:
        idx_ref, src_ref, out_ref, sem = refs
    out_hbm = out_ref
    i = pl.program_id(0)

    if fill_rows:
        @pl.when(i == 0)
        def _():
            zbuf[...] = jnp.zeros(zbuf.shape, F32)
            n_fill = fill_ref[0]

            def start_fill(k, carry):
                start = pl.multiple_of(fill_ref[1 + k], fill_rows)
                pltpu.make_async_copy(zbuf, out_hbm.at[pl.ds(start, fill_rows), :], zsem).start()
                return carry

            def wait_fill(k, carry):
                pltpu.make_async_copy(zbuf, out_hbm.at[pl.ds(0, fill_rows), :], zsem).wait()
                return carry

            lax.fori_loop(0, n_fill, start_fill, 0)
            lax.fori_loop(0, n_fill, wait_fill, 0)

    group = 16

    def body(j, carry):
        r0 = pl.multiple_of(j * group, group)
        for k in range(group):
            p = idx_ref[0, 0, r0 + k]
            if scatter:
                cp = pltpu.make_async_copy(src_ref.at[pl.ds(r0 + k, 1), :], out_ref.at[pl.ds(p, 1), :], sem)
            else:
                cp = pltpu.make_async_copy(src_ref.at[pl.ds(p, 1), :], out_ref.at[pl.ds(r0 + k, 1), :], sem)
            cp.start()
        return carry

    lax.fori_loop(0, tn // group, body, 0)
    if scatter:
        pltpu.make_async_copy(src_ref, out_ref.at[pl.ds(0, tn), :], sem).wait()
    else:
        pltpu.make_async_copy(src_ref.at[pl.ds(0, tn), :], out_ref, sem).wait()


def _permute_rows(src, idx, *, n_rows, n_out, scatter, fill=None, fill_rows=0):
    w = src.shape[1]
    tn = _pick_tile(n_rows, 1024)
    steps = n_rows // tn
    idx3 = idx.reshape(steps, 1, tn)
    block = pl.BlockSpec((tn, w), lambda i, *_: (i, 0))
    hbm = pl.BlockSpec(memory_space=pl.ANY)
    in_specs = [pl.BlockSpec((1, 1, tn), lambda i, *_: (i, 0, 0), memory_space=pltpu.SMEM),
                block if scatter else hbm]
    scratch = [pltpu.SemaphoreType.DMA(())]
    args = [idx3, src]
    n_prefetch = 0
    if fill_rows:
        n_prefetch = 1
        args = [fill] + args
        scratch = [pltpu.VMEM((fill_rows, w), F32), pltpu.SemaphoreType.DMA(()), pltpu.SemaphoreType.DMA(())]
    return pl.pallas_call(
        functools.partial(_permute_kernel, tn=tn, scatter=scatter, fill_rows=fill_rows),
        grid_spec=pltpu.PrefetchScalarGridSpec(
            num_scalar_prefetch=n_prefetch, grid=(steps,), in_specs=in_specs,
            out_specs=hbm if scatter else block, scratch_shapes=scratch),
        out_shape=jax.ShapeDtypeStruct((n_out, w), F32),
        compiler_params=_cparams(("arbitrary",)),
        name="permute_scatter" if scatter else "permute_gather",
    )(*args)


def _moe_kernel(elo_ref, ehi_ref, nused_ref, xs_ref, g2_ref, wg0_ref, wu0_ref, wd0_ref, wg1_ref, wu1_ref, wd1_ref,
                fnw_ref, o_ref, *, tm, final, n_mod):
    used = pl.program_id(0) < nused_ref[0]

    @pl.when(jnp.logical_not(used))
    def _():
        o_ref[...] = jnp.zeros(o_ref.shape, F32)

    @pl.when(used)
    def _():
        rows = xs_ref[...]
        h = rows[:, ROW_H:ROW_H + D_MODEL].astype(BF16)
        y = None
        for k, (wg_ref, wu_ref, wd_ref) in enumerate(((wg0_ref, wu0_ref, wd0_ref), (wg1_ref, wu1_ref, wd1_ref))):
            act = _silu(_dot(h, wg_ref[0])) * _dot(h, wu_ref[0])
            gate = rows[:, ROW_AUX + k:ROW_AUX + k + 1]
            part = _dot((act * gate).astype(BF16), wd_ref[0])
            y = part if y is None else y + part
        if n_mod == 1:
            g2 = g2_ref[0:1, :]
        else:
            mod_row = rows[:, ROW_AUX + AUX_MODROW:ROW_AUX + AUX_MODROW + 1]
            ids = lax.broadcasted_iota(jnp.int32, (tm, g2_ref.shape[0]), 1).astype(F32)
            onehot = jnp.where(mod_row == ids, 1.0, 0.0).astype(BF16)
            g_hi, g_lo = _split2(g2_ref[...])
            g2 = _dot(onehot, g_hi) + _dot(onehot, g_lo)
        xn = rows[:, ROW_X:ROW_X + D_MODEL] + g2 * y
        if final:
            xn = _rms(xn) * fnw_ref[...]
        o_ref[...] = xn


def _moe(rows, gid, g2, lw, final_norm_w, *, n_tokens, final):
    t, d, tm = n_tokens, D_MODEL, MOE_TILE
    dest, tile_lo, tile_hi, n_used, fill = _moe_plan(gid.reshape(t), t, tm)
    nt = tile_lo.shape[0]
    n_sorted = nt * tm
    n_mod = g2.shape[0]
    if n_mod > 1:
        g2 = jnp.pad(g2, ((0, (-n_mod) % 8), (0, 0)))
    xs = _permute_rows(rows, dest, n_rows=t, n_out=n_sorted, scatter=True, fill=fill, fill_rows=tm)

    def tile(i, nu):
        return jnp.minimum(i, nu[0] - 1)

    def expert_specs(which):
        def idx(i, elo, ehi, nu):
            return ((elo, ehi)[which][tile(i, nu)], 0, 0)
        return [pl.BlockSpec((1, d, D_EXPERT), idx), pl.BlockSpec((1, d, D_EXPERT), idx),
                pl.BlockSpec((1, D_EXPERT, d), idx)]

    grid_spec = pltpu.PrefetchScalarGridSpec(
        num_scalar_prefetch=3,
        grid=(nt,),
        in_specs=[pl.BlockSpec((tm, ROW_W), lambda i, elo, ehi, nu: (tile(i, nu), 0)),
                  pl.BlockSpec(g2.shape, lambda i, elo, ehi, nu: (0, 0))]
                 + expert_specs(0) + expert_specs(1)
                 + [pl.BlockSpec((1, d), lambda i, elo, ehi, nu: (0, 0))],
        out_specs=pl.BlockSpec((tm, d), lambda i, elo, ehi, nu: (i, 0)))
    we = (lw["w_gate_e"], lw["w_up_e"], lw["w_down_e"])
    ys = pl.pallas_call(
        functools.partial(_moe_kernel, tm=tm, final=final, n_mod=n_mod),
        grid_spec=grid_spec,
        out_shape=jax.ShapeDtypeStruct((n_sorted, d), F32),
        compiler_params=_cparams(("arbitrary",)),
        name="moe",
    )(tile_lo, tile_hi, n_used, xs, g2, *we, *we, final_norm_w)
    return _permute_rows(ys, dest, n_rows=t, n_out=t, scatter=False)


def _grid_pos_embed(n_tokens, dim):
    rows = n_tokens // GRID_W
    quarter = dim // 4
    omega = 1.0 / (10000.0 ** (jnp.arange(quarter, dtype=F32) / quarter))
    r = jnp.arange(rows, dtype=F32)[:, None] * omega
    col = jnp.arange(GRID_W, dtype=F32)[:, None] * omega
    r_emb = jnp.concatenate([jnp.sin(r), jnp.cos(r)], axis=-1)
    c_emb = jnp.concatenate([jnp.sin(col), jnp.cos(col)], axis=-1)
    emb = jnp.concatenate([
        jnp.broadcast_to(r_emb[:, None, :], (rows, GRID_W, dim // 2)),
        jnp.broadcast_to(c_emb[None, :, :], (rows, GRID_W, dim // 2))], axis=-1)
    return emb.reshape(rows * GRID_W, dim)


def _dt_perm():
    return np.array([dr * SSD_HEADS + g * SSD_HPG + j
                     for g in range(SSD_GROUPS) for dr in range(2) for j in range(SSD_HPG)], np.int32)


def _layer_weights(l, w_in, norm1_w, gmlp_norm_w, gmlp_ws, gmlp_bs, conv_w, conv_b, dt_bias, a_log, d_skip,
                   ssd_norm_w, w_branch_a, w_branch_b, w_out, norm2_w, router_w, router_bias,
                   w_gate_e, w_up_e, w_down_e):
    perm = _dt_perm()
    w_dt = jnp.pad(w_in[l][:, MAIN_COLS:][:, perm], ((0, 0), (0, DT_PAD - 2 * SSD_HEADS)))
    prow = jnp.stack([dt_bias[l].reshape(-1)[perm], a_log[l].reshape(-1)[perm]], axis=0)
    prow = prow.reshape(2, SSD_GROUPS, 16).transpose(1, 0, 2)
    return {
        "norm1_w": norm1_w[l][None],
        "w_main": w_in[l][:, :MAIN_COLS].astype(BF16),
        "w_dt": w_dt.astype(BF16),
        "gmlp_norm_w": gmlp_norm_w[l][None],
        "gmlp_ws": gmlp_ws[l].astype(BF16),
        "gmlp_bs_x": jnp.repeat(gmlp_bs[l].T, GMLP_WIDTH // GMLP_GROUPS, axis=1),
        "conv_w": conv_w[l],
        "conv_b": conv_b[l][None],
        "ssd_pcol": prow.transpose(0, 2, 1),
        "d_skip_x": jnp.repeat(d_skip[l], SSD_HEADDIM)[None],
        "ssd_norm_w": ssd_norm_w[l][None],
        "w_branch_a": w_branch_a[l].astype(BF16),
        "w_branch_b": w_branch_b[l].astype(BF16),
        "w_out": w_out[l].astype(BF16),
        "norm2_w": norm2_w[l][None],
        "router_wt": router_w.T,
        "router_bias": router_bias[:, None],
        "w_gate_e": w_gate_e[l].astype(BF16),
        "w_up_e": w_up_e[l].astype(BF16),
        "w_down_e": w_down_e[l].astype(BF16),
    }


def _layer(x, pos, mod, lw, h0, prev_state, final_norm_w, *, batch, seq_len, layer, emit_state, final):
    t = batch * seq_len
    proj, dtr = _inproj(x, pos, mod, lw["norm1_w"], lw["w_main"], lw["w_dt"], n_tokens=t)
    yb, state = _ssd(proj, dtr, lw, h0, prev_state, batch=batch, seq_len=seq_len, layer=layer,
                     emit_state=emit_state)
    rows, gid = _mixout(x, pos, proj, yb, mod, lw, n_tokens=t)
    out = _moe(rows, gid, mod[:, N_MOD - 1, :], lw, final_norm_w, n_tokens=t, final=final)
    return out, state


def kernel(x_prompt, x_sample, state_ssd, c, c_ctx, w_mod, b_mod, norm1_w, w_in, gmlp_norm_w, gmlp_ws, gmlp_bs, conv_w, conv_b, dt_bias, a_log, d_skip, ssd_norm_w, w_branch_a, w_branch_b, w_out, norm2_w, router_w, router_bias, w_gate_e, w_up_e, w_down_e, final_norm_w):
    bp, lp, d = x_prompt.shape
    bs, ls, _ = x_sample.shape
    depth = w_mod.shape[0]
    assert 1 + bs <= MOD_ROWS and d == D_MODEL
    assert lp % (2 * CHUNK) == 0 and ls % (2 * CHUNK) == 0

    cond = jnp.concatenate([c_ctx[None], c, jnp.zeros((MOD_ROWS - 1 - bs, d), F32)], axis=0)
    mod = _modulation(cond, w_mod, b_mod).reshape(depth, MOD_ROWS, N_MOD, d)

    pos = _grid_pos_embed(ls, d)
    xp = x_prompt.reshape(bp * lp, d)
    xs = x_sample.reshape(bs * ls, d)
    cache = state_ssd.reshape(bs, depth, 2, SSD_GROUPS, GROUP_CH, SSD_STATE)
    fnw = final_norm_w[None]

    states = None
    for l in range(depth):
        lw = _layer_weights(l, w_in, norm1_w, gmlp_norm_w, gmlp_ws, gmlp_bs, conv_w, conv_b, dt_bias, a_log,
                            d_skip, ssd_norm_w, w_branch_a, w_branch_b, w_out, norm2_w, router_w, router_bias,
                            w_gate_e, w_up_e, w_down_e)
        final = l == depth - 1
        xp, states = _layer(xp, None, mod[l, 0:1], lw, None, states, fnw, batch=bp, seq_len=lp, layer=l,
                            emit_state=True, final=final)
        xs, _ = _layer(xs, pos if l == 0 else None, mod[l, 1:1 + bs], lw, cache, None, fnw, batch=bs, seq_len=ls,
                       layer=l, emit_state=False, final=final)

    new_state = states.reshape(bp, depth, 2, SSD_HEADS, SSD_HEADDIM, SSD_STATE)
    return (xp.reshape(bp, lp, d), xs.reshape(bs, ls, d), new_state.astype(x_prompt.dtype))
```

```python
import functools
import math

import numpy as np
import jax
import jax.numpy as jnp
from jax import lax
from jax.experimental import pallas as pl
from jax.experimental.pallas import tpu as pltpu

F32 = jnp.float32
BF16 = jnp.bfloat16

D_MODEL = 1024
CHUNK = 128
GRID_W = 64
GMLP_WIDTH = 1024
GMLP_GROUPS = 8
SSD_INNER = 2048
SSD_HEADDIM = 64
SSD_HEADS = 32
SSD_GROUPS = 4
SSD_HPG = 8
SSD_STATE = 128
SSD_CONV = 5
GROUP_CH = SSD_INNER // SSD_GROUPS
N_EXPERTS = 16
N_EXPERT_GROUPS = 4
EXPERTS_PER_GROUP = 4
D_EXPERT = 512
N_MOD = 6
EPS = 1e-6
MAIN_COLS = 2 * D_MODEL + 2 * GMLP_WIDTH + SSD_INNER + SSD_INNER + 2 * SSD_GROUPS * SSD_STATE
COL_Z = 2 * D_MODEL + 2 * GMLP_WIDTH
COL_X = COL_Z + SSD_INNER
COL_B = COL_X + SSD_INNER
COL_C = COL_B + SSD_GROUPS * SSD_STATE
DT_PAD = 128
ROW_X = 0
ROW_H = D_MODEL
ROW_AUX = 2 * D_MODEL
PAIRS_PER_GROUP = 6
N_BUCKETS = 24
ROW_W = 2 * D_MODEL + 128
AUX_MODROW = 4
MOE_TILE = 256
MIX_SUB = 128
MIX_TILE = 512
MOD_ROWS = 16

V7X_VMEM_LIMIT = 56 * 1024 * 1024


def _cparams(sem):
    return pltpu.CompilerParams(dimension_semantics=sem, vmem_limit_bytes=V7X_VMEM_LIMIT)


def _pick_tile(rows, preferred):
    tm = preferred
    while rows % tm:
        tm //= 2
    assert tm >= CHUNK
    return tm


def _split3(a):
    hi = a.astype(BF16)
    r1 = a - hi.astype(F32)
    mid = r1.astype(BF16)
    lo = (r1 - mid.astype(F32)).astype(BF16)
    return hi, mid, lo


def _split2(a):
    hi = a.astype(BF16)
    lo = (a - hi.astype(F32)).astype(BF16)
    return hi, lo


def _dot(a, b):
    return jnp.dot(a, b, preferred_element_type=F32)


def _dot_nt(a, b):
    return lax.dot_general(a, b, (((1,), (1,)), ((), ())), preferred_element_type=F32)


def _silu(x):
    return x * jax.nn.sigmoid(x)


def _gelu_tanh(x):
    c = math.sqrt(2.0 / math.pi)
    return x * (0.5 * (1.0 + jnp.tanh(c * (x + 0.044715 * (x * x * x)))))


def _softplus(x):
    return jnp.maximum(x, 0.0) + jnp.log1p(jnp.exp(-jnp.abs(x)))


def _rms(x):
    return x * lax.rsqrt(jnp.mean(x * x, axis=-1, keepdims=True) + EPS)


def _mod_kernel(cond_ref, w_ref, b_ref, o_ref):
    c = cond_ref[...]
    s_hi, s_lo = _split2(_silu(c))
    w_hi, w_lo = _split2(w_ref[0])
    o_ref[0] = _dot(s_hi, w_hi) + _dot(s_hi, w_lo) + _dot(s_lo, w_hi) + b_ref[0]


def _modulation(cond, w_mod, b_mod):
    depth, d, n = w_mod.shape
    tn = 1536
    return pl.pallas_call(
        _mod_kernel,
        grid=(depth, n // tn),
        in_specs=[pl.BlockSpec((MOD_ROWS, d), lambda l, j: (0, 0)),
                  pl.BlockSpec((1, d, tn), lambda l, j: (l, 0, j)),
                  pl.BlockSpec((1, 1, tn), lambda l, j: (l, 0, j))],
        out_specs=pl.BlockSpec((1, MOD_ROWS, tn), lambda l, j: (l, 0, j)),
        out_shape=jax.ShapeDtypeStruct((depth, MOD_ROWS, n), F32),
        compiler_params=_cparams(("arbitrary", "arbitrary")),
        name="modulation",
    )(cond, w_mod, b_mod.reshape(depth, 1, n))


def _inproj_kernel(*refs, has_pos, tm):
    if has_pos:
        x_ref, pos_ref, mod_ref, nw_ref, w_ref, wdt_ref, proj_ref, dtr_ref, h_scr = refs
    else:
        x_ref, mod_ref, nw_ref, w_ref, wdt_ref, proj_ref, dtr_ref, h_scr = refs
        pos_ref = None

    @pl.when(pl.program_id(1) == 0)
    def _():
        x = x_ref[...]
        if has_pos:
            x = x + pos_ref[...]
        h = (_rms(x) * nw_ref[...]) * (1.0 + mod_ref[0, 1:2, :]) + mod_ref[0, 0:1, :]
        hb = h.astype(BF16)
        h_scr[...] = hb
        dt = _dot(hb, wdt_ref[...])
        dtt = dt.T
        for g in range(SSD_GROUPS):
            dtr_ref[g] = dtt[g * 16:(g + 1) * 16, :]

    proj_ref[...] = _dot(h_scr[...], w_ref[...]).astype(BF16)


def _inproj(x, pos, mod, norm_w, w_main, w_dt, *, n_tokens):
    t, d = n_tokens, x.shape[1]
    tn = 1536
    nb = mod.shape[0]
    rows_per_mod = t // nb
    tm = _pick_tile(rows_per_mod, 1024)
    has_pos = pos is not None
    in_specs = [pl.BlockSpec((tm, d), lambda i, j: (i, 0))]
    args = [x]
    if has_pos:
        pos_blocks = pos.shape[0] // tm
        in_specs.append(pl.BlockSpec((tm, d), lambda i, j: (i % pos_blocks, 0)))
        args.append(pos)
    in_specs += [pl.BlockSpec((1, N_MOD, d), lambda i, j: ((i * tm) // rows_per_mod, 0, 0)),
                 pl.BlockSpec((1, d), lambda i, j: (0, 0)),
                 pl.BlockSpec((d, tn), lambda i, j: (0, j)),
                 pl.BlockSpec((d, DT_PAD), lambda i, j: (0, 0))]
    args += [mod, norm_w, w_main, w_dt]
    return pl.pallas_call(
        functools.partial(_inproj_kernel, has_pos=has_pos, tm=tm),
        grid=(t // tm, MAIN_COLS // tn),
        in_specs=in_specs,
        out_specs=[pl.BlockSpec((tm, tn), lambda i, j: (i, j)),
                   pl.BlockSpec((SSD_GROUPS, 16, tm), lambda i, j: (0, 0, i))],
        out_shape=[jax.ShapeDtypeStruct((t, MAIN_COLS), BF16),
                   jax.ShapeDtypeStruct((SSD_GROUPS, 16, t), F32)],
        scratch_shapes=[pltpu.VMEM((tm, d), BF16)],
        compiler_params=_cparams(("arbitrary", "arbitrary")),
        name="inproj",
    )(*args)


def _tri_dot_right(a, tri):
    hi, mid, lo = _split3(a)
    return _dot(hi, tri) + _dot(mid, tri) + _dot(lo, tri)


COL_ACS = 0
COL_E = 16
COL_W = 32
ROW_Q = 0
ROW_LD = 16


def _ssd_kernel(*refs, seq_len, has_h0, emit_state, n_prev):
    it = iter(refs)
    x_ref, b_ref, c_ref, z_ref, dtr_ref = (next(it) for _ in range(5))
    cwx_ref, cwb_ref, cwc_ref, cbx_ref, cbb_ref, cbc_ref = (next(it) for _ in range(6))
    pcol_ref, dsk_ref, nw_ref, fanf_ref, fanb_ref, bcast_ref = (next(it) for _ in range(6))
    h0_ref = next(it) if has_h0 else None
    prev_ref = next(it) if n_prev else None
    y_ref = next(it)
    st_ref = next(it) if emit_state else None
    (pad_scr, xs_scr, cbf_scr, bbf_scr, btf_scr, yb_scr, yf_scr, dall_scr, acs_scr, e_scr, w_scr, q_scr, ld_scr,
     rt_scr, col_scr, rq_scr, hf_scr, hb_scr, xbd_scr, yd_scr) = (next(it) for _ in range(20))

    L = seq_len
    nc = L // CHUNK
    W = GROUP_CH + 2 * SSD_STATE
    n_xblk = GROUP_CH // 128

    zeros8 = jnp.zeros((8, W), F32)
    pad_scr[0:8, :] = zeros8
    pad_scr[L + 8:L + 16, :] = zeros8

    def stage(c, carry):
        r0 = pl.multiple_of(c * CHUNK, CHUNK)
        pad_scr[pl.ds(r0 + 8, CHUNK), 0:GROUP_CH] = x_ref[0, pl.ds(r0, CHUNK), :].astype(F32)
        pad_scr[pl.ds(r0 + 8, CHUNK), GROUP_CH:GROUP_CH + SSD_STATE] = b_ref[0, pl.ds(r0, CHUNK), :].astype(F32)
        pad_scr[pl.ds(r0 + 8, CHUNK), GROUP_CH + SSD_STATE:W] = c_ref[0, pl.ds(r0, CHUNK), :].astype(F32)
        return carry

    lax.fori_loop(0, nc, stage, 0)

    cw = [cwx_ref[...][:, k * 128:(k + 1) * 128] for k in range(GROUP_CH // 128)] + [cwb_ref[...], cwc_ref[...]]
    cb = [cbx_ref[...][:, k * 128:(k + 1) * 128] for k in range(GROUP_CH // 128)] + [cbb_ref[...], cbc_ref[...]]
    win_rows = CHUNK + 16

    def conv(c, carry):
        r0 = pl.multiple_of(c * CHUNK, CHUNK)
        for k in range(W // 128):
            win = pad_scr[pl.ds(r0, win_rows), k * 128:(k + 1) * 128]
            acc = win[8:8 + CHUNK, :] * cw[k][2:3, :]
            for tap in (0, 1, 3, 4):
                d = tap - 2
                rolled = pltpu.roll(win, (win_rows - d) % win_rows, 0)
                acc = acc + rolled[8:8 + CHUNK, :] * cw[k][tap:tap + 1, :]
            acc = _silu(acc + cb[k])
            if k < n_xblk:
                xs_scr[pl.ds(r0, CHUNK), k * 128:(k + 1) * 128] = acc
            elif k == n_xblk:
                bbf_scr[pl.ds(r0, CHUNK), :] = acc.astype(BF16)
                btf_scr[c] = acc.T.astype(BF16)
            else:
                cbf_scr[pl.ds(r0, CHUNK), :] = acc.astype(BF16)
        return carry

    lax.fori_loop(0, nc, conv, 0)

    ri = lax.broadcasted_iota(jnp.int32, (CHUNK, CHUNK), 0)
    ci = lax.broadcasted_iota(jnp.int32, (CHUNK, CHUNK), 1)
    below = ci < ri
    above = ci > ri
    tri_ge = jnp.where(ci >= ri, 1.0, 0.0).astype(BF16)
    tri_le = jnp.where(ci <= ri, 1.0, 0.0).astype(BF16)
    left_half = ci < SSD_HEADDIM

    bias_col = pcol_ref[0, :, 0:1]
    aneg_col = -jnp.exp(pcol_ref[0, :, 1:2])
    for c in range(nc):
        dall_scr[c * 16:(c + 1) * 16, :] = _softplus(dtr_ref[0, :, c * CHUNK:(c + 1) * CHUNK] + bias_col)
    d_all = dall_scr[...]
    nr = nc * 16
    a_all = d_all * jnp.concatenate([aneg_col] * nc, axis=0)
    fwd_rows = (lax.broadcasted_iota(jnp.int32, (nr, 1), 0) & SSD_HPG) == 0
    acs = jnp.where(fwd_rows, _tri_dot_right(a_all, tri_ge), _tri_dot_right(a_all, tri_le))
    total = jnp.where(fwd_rows, acs[:, CHUNK - 1:CHUNK], acs[:, 0:1])
    acs_scr[...] = acs
    e_scr[...] = jnp.exp(acs)
    w_scr[...] = d_all * jnp.exp(total - acs)
    q_scr[...] = acs - jnp.log(d_all)
    ld_scr[...] = jnp.log(d_all + pltpu.roll(d_all, nr - SSD_HPG, 0))

    rt_scr[...] = jnp.zeros(rt_scr.shape, F32)
    for c in range(nc):
        rows = slice(c * 16, (c + 1) * 16)
        rt_scr[COL_ACS:COL_ACS + 16, :] = acs_scr[rows, :]
        rt_scr[COL_E:COL_E + 16, :] = e_scr[rows, :]
        rt_scr[COL_W:COL_W + 16, :] = w_scr[rows, :]
        col_scr[c] = rt_scr[...].T
        rq_scr[c, ROW_Q:ROW_Q + 16, :] = q_scr[rows, :]
        rq_scr[c, ROW_LD:ROW_LD + SSD_HPG, :] = ld_scr[c * 16:c * 16 + SSD_HPG, :]

    def fans(col, fan_ref):
        hi, lo = _split2(col)
        both = _dot(jnp.concatenate([hi, lo], axis=1), fan_ref[...])
        return both[:, 0:GROUP_CH], both[:, GROUP_CH:2 * GROUP_CH]

    if has_h0:
        hf_scr[...] = h0_ref[0, 0, 0, 0].T
        hb_scr[...] = h0_ref[0, 0, 1, 0].T
    else:
        hf_scr[...] = jnp.zeros((SSD_STATE, GROUP_CH), F32)
        hb_scr[...] = jnp.zeros((SSD_STATE, GROUP_CH), F32)
    xbd_scr[...] = jnp.zeros(xbd_scr.shape, BF16)

    n_half = GROUP_CH // 256

    def sweeps(i, carry):
        passes = ((nc - 1 - i, hb_scr, fanb_ref, yb_scr, 0), (i, hf_scr, fanf_ref, yf_scr, CHUNK - 1))
        staged = []
        for c, h_scr, fan_ref, y_scr, total_row in passes:
            r0 = pl.multiple_of(c * CHUNK, CHUNK)
            e_x, w_x = fans(col_scr[c], fan_ref)
            staged.append((r0, cbf_scr[pl.ds(r0, CHUNK), :], btf_scr[c], e_x, w_x))
        for hf in range(n_half):
            sl = slice(hf * 256, (hf + 1) * 256)
            for (c, h_scr, fan_ref, y_scr, total_row), (r0, cm, bt, e_x, w_x) in zip(passes, staged):
                e2 = e_x[:, sl]
                h = h_scr[:, sl]
                y_scr[pl.ds(r0, CHUNK), sl] = _dot(cm, h.astype(BF16)) * e2
                xw = (xs_scr[pl.ds(r0, CHUNK), sl] * w_x[:, sl]).astype(BF16)
                h_scr[:, sl] = h * e2[total_row:total_row + 1, :] + _dot(bt, xw)
        return carry

    lax.fori_loop(0, nc, sweeps, 0)

    def finish(j, carry):
        work = []
        for u in range(2):
            c = 2 * j + u
            r0 = pl.multiple_of(c * CHUNK, CHUNK)
            cb_mat = _dot_nt(cbf_scr[pl.ds(r0, CHUNK), :], bbf_scr[pl.ds(r0, CHUNK), :])
            hi, lo = _split2(col_scr[c])
            acs_b = _dot(jnp.concatenate([hi, lo], axis=1), bcast_ref[...])
            work.append((u, r0, acs_b, rq_scr[c], cb_mat))

        for u, r0, col, rq, cb_mat in work:
            for p in range(n_xblk):
                xb = xs_scr[pl.ds(r0, CHUNK), p * 128:(p + 1) * 128].astype(BF16)
                xbd_scr[u, p, 0:CHUNK, 0:SSD_HEADDIM] = xb[:, 0:SSD_HEADDIM]
                xbd_scr[u, p, CHUNK:2 * CHUNK, SSD_HEADDIM:2 * SSD_HEADDIM] = xb[:, SSD_HEADDIM:2 * SSD_HEADDIM]
        for p in range(n_xblk):
            for u, r0, col, rq, cb_mat in work:
                wpair = []
                for hh in (2 * p, 2 * p + 1):
                    pf = col[:, hh * CHUNK:(hh + 1) * CHUNK]
                    pb = col[:, (SSD_HPG + hh) * CHUNK:(SSD_HPG + hh + 1) * CHUNK]
                    qf = rq[ROW_Q + hh:ROW_Q + hh + 1, :]
                    qb = rq[ROW_Q + SSD_HPG + hh:ROW_Q + SSD_HPG + hh + 1, :]
                    ld = rq[ROW_LD + hh:ROW_LD + hh + 1, :]
                    arg = jnp.where(below, pf - qf, jnp.where(above, pb - qb, ld))
                    wpair.append((cb_mat * jnp.exp(arg)).astype(BF16))
                yd_scr[u, :, p * 128:(p + 1) * 128] = _dot(jnp.concatenate(wpair, axis=1), xbd_scr[u, p])

        for u, r0, col, rq, cb_mat in work:
            ssq = jnp.zeros((CHUNK, 1), F32)
            for hf in range(n_half):
                sl = slice(hf * 256, (hf + 1) * 256)
                y = (yb_scr[pl.ds(r0, CHUNK), sl] + yf_scr[pl.ds(r0, CHUNK), sl] + yd_scr[u, :, sl]
                     + xs_scr[pl.ds(r0, CHUNK), sl] * dsk_ref[:, sl])
                yg = y * _silu(z_ref[0, pl.ds(r0, CHUNK), sl].astype(F32))
                yd_scr[u, :, sl] = yg
                ssq = ssq + jnp.sum(yg * yg, axis=1, keepdims=True)
            scale = lax.rsqrt(ssq * (1.0 / GROUP_CH) + EPS)
            for hf in range(n_half):
                sl = slice(hf * 256, (hf + 1) * 256)
                y_ref[0, pl.ds(r0, CHUNK), sl] = (yd_scr[u, :, sl] * scale * nw_ref[:, sl]).astype(BF16)
        return carry

    lax.fori_loop(0, nc // 2, finish, 0)

    if emit_state:
        for l in range(n_prev):
            st_ref[0, l] = prev_ref[0, l]
        st_ref[0, n_prev, 0, 0] = hf_scr[...].T
        st_ref[0, n_prev, 1, 0] = hb_scr[...].T


def _fan_matrix(first_head):
    m = np.zeros((2 * CHUNK, 2 * GROUP_CH), np.float32)
    for piece in range(2):
        for part, base in enumerate((COL_E, COL_W)):
            for j in range(SSD_HPG):
                m[piece * CHUNK + base + first_head + j,
                  part * GROUP_CH + j * SSD_HEADDIM:part * GROUP_CH + (j + 1) * SSD_HEADDIM] = 1.0
    return jnp.asarray(m, BF16)


def _bcast_matrix():
    m = np.zeros((2 * CHUNK, 2 * SSD_HPG * CHUNK), np.float32)
    for piece in range(2):
        for j in range(2 * SSD_HPG):
            m[piece * CHUNK + COL_ACS + j, j * CHUNK:(j + 1) * CHUNK] = 1.0
    return jnp.asarray(m, BF16)


def _ssd(proj, dtr, lw, h0, prev_state, *, batch, seq_len, layer, emit_state):
    L = seq_len
    proj3 = proj.reshape(batch, L, MAIN_COLS)
    has_h0 = h0 is not None
    gc = GROUP_CH
    W = gc + 2 * SSD_STATE

    in_specs = [
        pl.BlockSpec((1, L, gc), lambda b, g: (b, 0, COL_X // gc + g)),
        pl.BlockSpec((1, L, SSD_STATE), lambda b, g: (b, 0, COL_B // SSD_STATE + g)),
        pl.BlockSpec((1, L, SSD_STATE), lambda b, g: (b, 0, COL_C // SSD_STATE + g)),
        pl.BlockSpec((1, L, gc), lambda b, g: (b, 0, COL_Z // gc + g)),
        pl.BlockSpec((1, 16, L), lambda b, g: (g, 0, b)),
        pl.BlockSpec((SSD_CONV, gc), lambda b, g: (0, g)),
        pl.BlockSpec((SSD_CONV, SSD_STATE), lambda b, g: (0, SSD_INNER // SSD_STATE + g)),
        pl.BlockSpec((SSD_CONV, SSD_STATE), lambda b, g: (0, SSD_INNER // SSD_STATE + SSD_GROUPS + g)),
        pl.BlockSpec((1, gc), lambda b, g: (0, g)),
        pl.BlockSpec((1, SSD_STATE), lambda b, g: (0, SSD_INNER // SSD_STATE + g)),
        pl.BlockSpec((1, SSD_STATE), lambda b, g: (0, SSD_INNER // SSD_STATE + SSD_GROUPS + g)),
        pl.BlockSpec((1, 16, 2), lambda b, g: (g, 0, 0)),
        pl.BlockSpec((1, gc), lambda b, g: (0, g)),
        pl.BlockSpec((1, gc), lambda b, g: (0, g)),
        pl.BlockSpec((2 * CHUNK, 2 * gc), lambda b, g: (0, 0)),
        pl.BlockSpec((2 * CHUNK, 2 * gc), lambda b, g: (0, 0)),
        pl.BlockSpec((2 * CHUNK, 2 * SSD_HPG * CHUNK), lambda b, g: (0, 0)),
    ]
    args = [proj3, proj3, proj3, proj3, dtr,
            lw["conv_w"], lw["conv_w"], lw["conv_w"], lw["conv_b"], lw["conv_b"], lw["conv_b"],
            lw["ssd_pcol"], lw["d_skip_x"], lw["ssd_norm_w"], _fan_matrix(0), _fan_matrix(SSD_HPG),
            _bcast_matrix()]
    if has_h0:
        in_specs.append(pl.BlockSpec((1, 1, 2, 1, gc, SSD_STATE), lambda b, g: (b, layer, 0, g, 0, 0)))
        args.append(h0)
    out_specs = [pl.BlockSpec((1, L, gc), lambda b, g: (b, 0, g))]
    out_shape = [jax.ShapeDtypeStruct((batch, L, SSD_INNER), BF16)]
    n_prev = 0 if prev_state is None else prev_state.shape[1]
    if n_prev:
        in_specs.append(pl.BlockSpec((1, n_prev, 2, 1, gc, SSD_STATE), lambda b, g: (b, 0, 0, g, 0, 0)))
        args.append(prev_state)
    if emit_state:
        out_specs.append(pl.BlockSpec((1, n_prev + 1, 2, 1, gc, SSD_STATE), lambda b, g: (b, 0, 0, g, 0, 0)))
        out_shape.append(jax.ShapeDtypeStruct((batch, n_prev + 1, 2, SSD_GROUPS, gc, SSD_STATE), F32))
    nc = L // CHUNK
    scratch = [
        pltpu.VMEM((L + 16, W), F32),
        pltpu.VMEM((L, gc), F32),
        pltpu.VMEM((L, SSD_STATE), BF16),
        pltpu.VMEM((L, SSD_STATE), BF16),
        pltpu.VMEM((nc, SSD_STATE, CHUNK), BF16),
        pltpu.VMEM((L, gc), F32),
        pltpu.VMEM((L, gc), F32),
        pltpu.VMEM((nc * 16, CHUNK), F32),
        pltpu.VMEM((nc * 16, CHUNK), F32),
        pltpu.VMEM((nc * 16, CHUNK), F32),
        pltpu.VMEM((nc * 16, CHUNK), F32),
        pltpu.VMEM((nc * 16, CHUNK), F32),
        pltpu.VMEM((nc * 16, CHUNK), F32),
        pltpu.VMEM((CHUNK, CHUNK), F32),
        pltpu.VMEM((nc, CHUNK, CHUNK), F32),
        pltpu.VMEM((nc, 24, CHUNK), F32),
        pltpu.VMEM((SSD_STATE, gc), F32),
        pltpu.VMEM((SSD_STATE, gc), F32),
        pltpu.VMEM((2, gc // 128, 2 * CHUNK, 128), BF16),
        pltpu.VMEM((2, CHUNK, gc), F32),
    ]
    outs = pl.pallas_call(
        functools.partial(_ssd_kernel, seq_len=L, has_h0=has_h0, emit_state=emit_state, n_prev=n_prev),
        grid=(batch, SSD_GROUPS),
        in_specs=in_specs,
        out_specs=out_specs,
        out_shape=out_shape,
        scratch_shapes=scratch,
        compiler_params=_cparams(("arbitrary", "arbitrary")),
        name="ssd",
    )(*args)
    y = outs[0].reshape(batch * L, SSD_INNER)
    return y, (outs[1] if emit_state else None)


def _route(sel, scores):
    neg = -jnp.inf

    def first_max(vals):
        m = vals[0]
        for v in vals[1:]:
            m = jnp.maximum(m, v)
        taken = jnp.zeros_like(m)
        flags = []
        for v in vals:
            f = jnp.where(v == m, 1.0, 0.0) * (1.0 - taken)
            flags.append(f)
            taken = taken + f
        return m, flags

    group_scores = []
    for j in range(N_EXPERT_GROUPS):
        a = sel[j * EXPERTS_PER_GROUP:(j + 1) * EXPERTS_PER_GROUP]
        m1, f1 = first_max(a)
        m2, _ = first_max([jnp.where(f > 0.5, neg, v) for f, v in zip(f1, a)])
        group_scores.append(m1 + m2)
    _, gflag = first_max(group_scores)
    masked = [jnp.where(gflag[e // EXPERTS_PER_GROUP] > 0.5, sel[e], neg) for e in range(N_EXPERTS)]
    _, f1 = first_max(masked)
    _, f2 = first_max([jnp.where(f > 0.5, neg, v) for f, v in zip(f1, masked)])
    w1 = sum(f * s for f, s in zip(f1, scores))
    w2 = sum(f * s for f, s in zip(f2, scores))
    tot = w1 + w2
    gates = [(f1[e] * w1 + f2[e] * w2) / tot for e in range(N_EXPERTS)]
    taken = jnp.zeros_like(tot)
    e_lo = e_hi = g_lo = g_hi = jnp.zeros_like(tot)
    for e in range(N_EXPERTS):
        sel_e = f1[e] + f2[e]
        low = sel_e * (1.0 - taken)
        high = sel_e - low
        taken = taken + sel_e
        e_lo = e_lo + float(e) * low
        e_hi = e_hi + float(e) * high
        g_lo = g_lo + low * gates[e]
        g_hi = g_hi + high * gates[e]
    group = sum(float(j) * gflag[j] for j in range(1, N_EXPERT_GROUPS))
    a = e_lo - EXPERTS_PER_GROUP * group
    b = e_hi - EXPERTS_PER_GROUP * group
    pair = a * (2 * EXPERTS_PER_GROUP - 1 - a) * 0.5 + (b - a - 1.0)
    bucket = group * PAIRS_PER_GROUP + pair
    return (g_lo, g_hi), bucket


def _mixout_kernel(*refs, has_pos, tm, rows_per_mod):
    it = iter(refs)
    x_ref = next(it)
    pos_ref = next(it) if has_pos else None
    ga_ref, gb_ref, u_ref, v_ref, yb_ref, mod_ref = (next(it) for _ in range(6))
    gnw_ref, ws_ref, bsx_ref, wa_ref, wb_ref, wo_ref, n2w_ref, rwt_ref, rb_ref = (next(it) for _ in range(9))
    rows_ref, gid_ref = (next(it) for _ in range(2))
    g_scr = next(it)

    sub = min(tm, MIX_SUB)
    spans = [slice(s * sub, (s + 1) * sub) for s in range(tm // sub)]
    n_chunks = tm // CHUNK
    gd = GMLP_WIDTH // GMLP_GROUPS

    u = [_gelu_tanh(u_ref[r, :].astype(F32)) for r in spans]
    vn = [(_rms(_gelu_tanh(v_ref[r, :].astype(F32))) * gnw_ref[...]).astype(BF16) for r in spans]

    def chunk_rows(c):
        s, local = divmod(c * CHUNK, sub)
        return s, slice(local, local + CHUNK)

    mixed_cols = [[None] * GMLP_GROUPS for _ in range(n_chunks)]
    for g in range(GMLP_GROUPS):
        pieces = []
        for c in range(n_chunks):
            s, rr = chunk_rows(c)
            pieces.append(vn[s][rr, g * gd:(g + 1) * gd])
        res = _dot(ws_ref[g], jnp.concatenate(pieces, axis=1))
        for c in range(n_chunks):
            mixed_cols[c][g] = res[:, c * gd:(c + 1) * gd]
    per_sub = sub // CHUNK
    ya = []
    for s in range(len(spans)):
        mixed = jnp.concatenate([jnp.concatenate(mixed_cols[s * per_sub + c], axis=1) + bsx_ref[...]
                                 for c in range(per_sub)], axis=0)
        ya.append((u[s] * mixed).astype(BF16))

    da = [_dot(y, wa_ref[...]) for y in ya]
    db = [_dot(yb_ref[r, :], wb_ref[...]) for r in spans]
    merged = [(jax.nn.sigmoid(ga_ref[r, :].astype(F32)) * a + jax.nn.sigmoid(gb_ref[r, :].astype(F32)) * b
               ).astype(BF16) for r, a, b in zip(spans, da, db)]
    do = [_dot(m, wo_ref[...]) for m in merged]

    h2s = []
    for r, o in zip(spans, do):
        x = x_ref[r, :]
        if has_pos:
            x = x + pos_ref[r, :]
        xn = x + mod_ref[0, 2:3, :] * o
        rows_ref[r, ROW_X:ROW_X + D_MODEL] = xn
        h2 = (_rms(xn) * n2w_ref[...]) * (1.0 + mod_ref[0, 4:5, :]) + mod_ref[0, 3:4, :]
        rows_ref[r, ROW_H:ROW_H + D_MODEL] = h2
        h2s.append(h2)

    r_hi, r_lo = _split2(rwt_ref[...])
    g_scr[...] = jnp.zeros(g_scr.shape, F32)
    mod_row = (pl.program_id(0) * tm) // rows_per_mod
    g_scr[AUX_MODROW:AUX_MODROW + 1, :] = jnp.full((1, tm), mod_row, jnp.int32).astype(F32)
    for r, h2 in zip(spans, h2s):
        h_hi, h_lo = _split2(h2)
        logits = _dot_nt(r_hi, h_hi) + _dot_nt(r_hi, h_lo) + _dot_nt(r_lo, h_hi)
        scores = jax.nn.sigmoid(logits)
        selm = scores + rb_ref[...]
        pair_gates, bucket = _route([selm[e:e + 1, :] for e in range(N_EXPERTS)],
                                    [scores[e:e + 1, :] for e in range(N_EXPERTS)])
        gid_ref[:, r] = bucket.astype(jnp.int32)
        for k in range(2):
            g_scr[k:k + 1, r] = pair_gates[k]
    rows_ref[:, ROW_AUX:ROW_W] = g_scr[...].T


def _mixout(x, pos, proj, yb, mod, lw, *, n_tokens):
    t, d = n_tokens, x.shape[1]
    nb = mod.shape[0]
    rows_per_mod = t // nb
    tm = _pick_tile(rows_per_mod, MIX_TILE)
    has_pos = pos is not None
    full = lambda shape: pl.BlockSpec(shape, lambda i: (0,) * len(shape), pipeline_mode=pl.Buffered(1))
    in_specs = [pl.BlockSpec((tm, d), lambda i: (i, 0))]
    args = [x]
    if has_pos:
        pos_blocks = pos.shape[0] // tm
        in_specs.append(pl.BlockSpec((tm, d), lambda i: (i % pos_blocks, 0)))
        args.append(pos)
    in_specs += [pl.BlockSpec((tm, d), lambda i: (i, 0)),
                 pl.BlockSpec((tm, d), lambda i: (i, 1)),
                 pl.BlockSpec((tm, d), lambda i: (i, 2)),
                 pl.BlockSpec((tm, d), lambda i: (i, 3)),
                 pl.BlockSpec((tm, SSD_INNER), lambda i: (i, 0)),
                 pl.BlockSpec((1, N_MOD, d), lambda i: ((i * tm) // rows_per_mod, 0, 0)),
                 full((1, GMLP_WIDTH)),
                 full((GMLP_GROUPS, CHUNK, CHUNK)),
                 full((CHUNK, GMLP_WIDTH)),
                 full((GMLP_WIDTH, d)),
                 full((SSD_INNER, d)),
                 full((d, d)),
                 full((1, d)),
                 full((N_EXPERTS, d)),
                 full((N_EXPERTS, 1))]
    args += [proj, proj, proj, proj, yb, mod,
             lw["gmlp_norm_w"], lw["gmlp_ws"], lw["gmlp_bs_x"], lw["w_branch_a"], lw["w_branch_b"], lw["w_out"],
             lw["norm2_w"], lw["router_wt"], lw["router_bias"]]
    return pl.pallas_call(
        functools.partial(_mixout_kernel, has_pos=has_pos, tm=tm, rows_per_mod=rows_per_mod),
        grid=(t // tm,),
        in_specs=in_specs,
        out_specs=[pl.BlockSpec((tm, ROW_W), lambda i: (i, 0)),
                   pl.BlockSpec((1, tm), lambda i: (0, i))],
        out_shape=[jax.ShapeDtypeStruct((t, ROW_W), F32),
                   jax.ShapeDtypeStruct((1, t), jnp.int32)],
        scratch_shapes=[pltpu.VMEM((ROW_W - ROW_AUX, tm), F32)],
        compiler_params=_cparams(("arbitrary",)),
        name="mixout",
    )(*args)


def _moe_plan(gid, n_tokens, tm):
    nb = N_BUCKETS
    t = n_tokens
    i32 = jnp.int32
    onehot = (gid[:, None] == jnp.arange(nb, dtype=i32)[None, :]).astype(i32)
    csum = jnp.cumsum(onehot, axis=0)
    rank = jnp.sum(onehot * (csum - 1), axis=1)
    counts = csum[-1]
    padded = ((counts + tm - 1) // tm) * tm
    ends = jnp.cumsum(padded)
    dest = jnp.sum(onehot * (ends - padded)[None, :], axis=1) + rank
    nt = pl.cdiv(t, tm) + nb
    tile_bucket = jnp.sum((jnp.arange(nt, dtype=i32) * tm)[:, None] >= ends[None, :], axis=1)
    tile_bucket = jnp.minimum(tile_bucket, nb - 1)
    pairs = [(a, b) for a in range(EXPERTS_PER_GROUP) for b in range(a + 1, EXPERTS_PER_GROUP)]
    lo = jnp.asarray([g * EXPERTS_PER_GROUP + a for g in range(N_EXPERT_GROUPS) for a, _ in pairs], i32)
    hi = jnp.asarray([g * EXPERTS_PER_GROUP + b for g in range(N_EXPERT_GROUPS) for _, b in pairs], i32)
    n_used = (ends[-1] // tm).astype(i32)
    tail = n_used + jnp.arange(nb, dtype=i32)
    cand = jnp.concatenate([jnp.where(padded > 0, ends - tm, -1), jnp.where(tail < nt, tail * tm, -1)])
    keep = cand >= 0
    order = jnp.argsort(jnp.logical_not(keep), stable=True)
    fill = jnp.concatenate([jnp.sum(keep).reshape(1), cand[order]]).astype(i32)
    return dest.astype(i32), lo[tile_bucket], hi[tile_bucket], n_used.reshape(1), fill


def _permute_kernel(*refs, tn, scatter, fill_rows):
    if fill_rows:
        fill_ref, idx_ref, src_ref, out_ref, zbuf, sem, zsem = refs
    else:
        idx_ref, src_ref, out_ref, sem = refs
    out_hbm = out_ref
    i = pl.program_id(0)

    if fill_rows:
        @pl.when(i == 0)
        def _():
            zbuf[...] = jnp.zeros(zbuf.shape, F32)
            n_fill = fill_ref[0]

            def start_fill(k, carry):
                start = pl.multiple_of(fill_ref[1 + k], fill_rows)
                pltpu.make_async_copy(zbuf, out_hbm.at[pl.ds(start, fill_rows), :], zsem).start()
                return carry

            def wait_fill(k, carry):
                pltpu.make_async_copy(zbuf, out_hbm.at[pl.ds(0, fill_rows), :], zsem).wait()
                return carry

            lax.fori_loop(0, n_fill, start_fill, 0)
            lax.fori_loop(0, n_fill, wait_fill, 0)

    group = 16

    def body(j, carry):
        r0 = pl.multiple_of(j * group, group)
        for k in range(group):
            p = idx_ref[0, 0, r0 + k]
            t8, s8 = j * (group // 8) + k // 8, k % 8
            if scatter:
                cp = pltpu.make_async_copy(src_ref.at[t8, pl.ds(s8, 1), :], out_ref.at[pl.ds(p, 1), :], sem)
            else:
                cp = pltpu.make_async_copy(src_ref.at[pl.ds(p, 1), :], out_ref.at[t8, pl.ds(s8, 1), :], sem)
            cp.start()
        return carry

    lax.fori_loop(0, tn // group, body, 0)
    block_ref = src_ref if scatter else out_ref
    pltpu.make_async_copy(block_ref, block_ref, sem).wait()


def _permute_rows(src, idx, *, n_rows, n_out, scatter, fill=None, fill_rows=0):
    w = src.shape[1]
    tn = _pick_tile(n_rows, 1024)
    steps = n_rows // tn
    idx3 = idx.reshape(steps, 1, tn)
    block = pl.BlockSpec((tn // 8, 8, w), lambda i, *_: (i, 0, 0))
    hbm = pl.BlockSpec(memory_space=pl.ANY)
    if scatter:
        src = src.reshape(src.shape[0] // 8, 8, w)
    in_specs = [pl.BlockSpec((1, 1, tn), lambda i, *_: (i, 0, 0), memory_space=pltpu.SMEM),
                block if scatter else hbm]
    scratch = [pltpu.SemaphoreType.DMA(())]
    args = [idx3, src]
    n_prefetch = 0
    if fill_rows:
        n_prefetch = 1
        args = [fill] + args
        scratch = [pltpu.VMEM((fill_rows, w), F32), pltpu.SemaphoreType.DMA(()), pltpu.SemaphoreType.DMA(())]
    out = pl.pallas_call(
        functools.partial(_permute_kernel, tn=tn, scatter=scatter, fill_rows=fill_rows),
        grid_spec=pltpu.PrefetchScalarGridSpec(
            num_scalar_prefetch=n_prefetch, grid=(steps,), in_specs=in_specs,
            out_specs=hbm if scatter else block, scratch_shapes=scratch),
        out_shape=jax.ShapeDtypeStruct((n_out, w) if scatter else (n_out // 8, 8, w), F32),
        compiler_params=_cparams(("arbitrary",)),
        name="permute_scatter" if scatter else "permute_gather",
    )(*args)
    return out if scatter else out.reshape(n_out, w)


def _moe_kernel(elo_ref, ehi_ref, nused_ref, xs_ref, g2_ref, wg0_ref, wu0_ref, wd0_ref, wg1_ref, wu1_ref, wd1_ref,
                fnw_ref, o_ref, *, tm, final, n_mod):
    used = pl.program_id(0) < nused_ref[0]

    @pl.when(jnp.logical_not(used))
    def _():
        o_ref[...] = jnp.zeros(o_ref.shape, F32)

    @pl.when(used)
    def _():
        rows = xs_ref[...]
        h = rows[:, ROW_H:ROW_H + D_MODEL].astype(BF16)
        y = None
        for k, (wg_ref, wu_ref, wd_ref) in enumerate(((wg0_ref, wu0_ref, wd0_ref), (wg1_ref, wu1_ref, wd1_ref))):
            act = _silu(_dot(h, wg_ref[0])) * _dot(h, wu_ref[0])
            gate = rows[:, ROW_AUX + k:ROW_AUX + k + 1]
            part = _dot((act * gate).astype(BF16), wd_ref[0])
            y = part if y is None else y + part
        if n_mod == 1:
            g2 = g2_ref[0:1, :]
        else:
            mod_row = rows[:, ROW_AUX + AUX_MODROW:ROW_AUX + AUX_MODROW + 1]
            ids = lax.broadcasted_iota(jnp.int32, (tm, g2_ref.shape[0]), 1).astype(F32)
            onehot = jnp.where(mod_row == ids, 1.0, 0.0).astype(BF16)
            g_hi, g_lo = _split2(g2_ref[...])
            g2 = _dot(onehot, g_hi) + _dot(onehot, g_lo)
        xn = rows[:, ROW_X:ROW_X + D_MODEL] + g2 * y
        if final:
            xn = _rms(xn) * fnw_ref[...]
        o_ref[...] = xn


def _moe(rows, gid, g2, lw, final_norm_w, *, n_tokens, final):
    t, d, tm = n_tokens, D_MODEL, MOE_TILE
    dest, tile_lo, tile_hi, n_used, fill = _moe_plan(gid.reshape(t), t, tm)
    nt = tile_lo.shape[0]
    n_sorted = nt * tm
    n_mod = g2.shape[0]
    if n_mod > 1:
        g2 = jnp.pad(g2, ((0, (-n_mod) % 8), (0, 0)))
    xs = _permute_rows(rows, dest, n_rows=t, n_out=n_sorted, scatter=True, fill=fill, fill_rows=tm)

    def tile(i, nu):
        return jnp.minimum(i, nu[0] - 1)

    def expert_specs(which):
        def idx(i, elo, ehi, nu):
            return ((elo, ehi)[which][tile(i, nu)], 0, 0)
        return [pl.BlockSpec((1, d, D_EXPERT), idx), pl.BlockSpec((1, d, D_EXPERT), idx),
                pl.BlockSpec((1, D_EXPERT, d), idx)]

    grid_spec = pltpu.PrefetchScalarGridSpec(
        num_scalar_prefetch=3,
        grid=(nt,),
        in_specs=[pl.BlockSpec((tm, ROW_W), lambda i, elo, ehi, nu: (tile(i, nu), 0)),
                  pl.BlockSpec(g2.shape, lambda i, elo, ehi, nu: (0, 0))]
                 + expert_specs(0) + expert_specs(1)
                 + [pl.BlockSpec((1, d), lambda i, elo, ehi, nu: (0, 0))],
        out_specs=pl.BlockSpec((tm, d), lambda i, elo, ehi, nu: (i, 0)))
    we = (lw["w_gate_e"], lw["w_up_e"], lw["w_down_e"])
    ys = pl.pallas_call(
        functools.partial(_moe_kernel, tm=tm, final=final, n_mod=n_mod),
        grid_spec=grid_spec,
        out_shape=jax.ShapeDtypeStruct((n_sorted, d), F32),
        compiler_params=_cparams(("arbitrary",)),
        name="moe",
    )(tile_lo, tile_hi, n_used, xs, g2, *we, *we, final_norm_w)
    return _permute_rows(ys, dest, n_rows=t, n_out=t, scatter=False)


def _grid_pos_embed(n_tokens, dim):
    rows = n_tokens // GRID_W
    quarter = dim // 4
    omega = 1.0 / (10000.0 ** (jnp.arange(quarter, dtype=F32) / quarter))
    r = jnp.arange(rows, dtype=F32)[:, None] * omega
    col = jnp.arange(GRID_W, dtype=F32)[:, None] * omega
    r_emb = jnp.concatenate([jnp.sin(r), jnp.cos(r)], axis=-1)
    c_emb = jnp.concatenate([jnp.sin(col), jnp.cos(col)], axis=-1)
    emb = jnp.concatenate([
        jnp.broadcast_to(r_emb[:, None, :], (rows, GRID_W, dim // 2)),
        jnp.broadcast_to(c_emb[None, :, :], (rows, GRID_W, dim // 2))], axis=-1)
    return emb.reshape(rows * GRID_W, dim)


def _dt_perm():
    return np.array([dr * SSD_HEADS + g * SSD_HPG + j
                     for g in range(SSD_GROUPS) for dr in range(2) for j in range(SSD_HPG)], np.int32)


def _layer_weights(l, w_in, norm1_w, gmlp_norm_w, gmlp_ws, gmlp_bs, conv_w, conv_b, dt_bias, a_log, d_skip,
                   ssd_norm_w, w_branch_a, w_branch_b, w_out, norm2_w, router_w, router_bias,
                   w_gate_e, w_up_e, w_down_e):
    perm = _dt_perm()
    w_dt = jnp.pad(w_in[l][:, MAIN_COLS:][:, perm], ((0, 0), (0, DT_PAD - 2 * SSD_HEADS)))
    prow = jnp.stack([dt_bias[l].reshape(-1)[perm], a_log[l].reshape(-1)[perm]], axis=0)
    prow = prow.reshape(2, SSD_GROUPS, 16).transpose(1, 0, 2)
    return {
        "norm1_w": norm1_w[l][None],
        "w_main": w_in[l][:, :MAIN_COLS].astype(BF16),
        "w_dt": w_dt.astype(BF16),
        "gmlp_norm_w": gmlp_norm_w[l][None],
        "gmlp_ws": gmlp_ws[l].astype(BF16),
        "gmlp_bs_x": jnp.repeat(gmlp_bs[l].T, GMLP_WIDTH // GMLP_GROUPS, axis=1),
        "conv_w": conv_w[l],
        "conv_b": conv_b[l][None],
        "ssd_pcol": prow.transpose(0, 2, 1),
        "d_skip_x": jnp.repeat(d_skip[l], SSD_HEADDIM)[None],
        "ssd_norm_w": ssd_norm_w[l][None],
        "w_branch_a": w_branch_a[l].astype(BF16),
        "w_branch_b": w_branch_b[l].astype(BF16),
        "w_out": w_out[l].astype(BF16),
        "norm2_w": norm2_w[l][None],
        "router_wt": router_w.T,
        "router_bias": router_bias[:, None],
        "w_gate_e": w_gate_e[l].astype(BF16),
        "w_up_e": w_up_e[l].astype(BF16),
        "w_down_e": w_down_e[l].astype(BF16),
    }


def _layer(x, pos, mod, lw, h0, prev_state, final_norm_w, *, batch, seq_len, layer, emit_state, final):
    t = batch * seq_len
    proj, dtr = _inproj(x, pos, mod, lw["norm1_w"], lw["w_main"], lw["w_dt"], n_tokens=t)
    yb, state = _ssd(proj, dtr, lw, h0, prev_state, batch=batch, seq_len=seq_len, layer=layer,
                     emit_state=emit_state)
    rows, gid = _mixout(x, pos, proj, yb, mod, lw, n_tokens=t)
    out = _moe(rows, gid, mod[:, N_MOD - 1, :], lw, final_norm_w, n_tokens=t, final=final)
    return out, state


def kernel(x_prompt, x_sample, state_ssd, c, c_ctx, w_mod, b_mod, norm1_w, w_in, gmlp_norm_w, gmlp_ws, gmlp_bs, conv_w, conv_b, dt_bias, a_log, d_skip, ssd_norm_w, w_branch_a, w_branch_b, w_out, norm2_w, router_w, router_bias, w_gate_e, w_up_e, w_down_e, final_norm_w):
    bp, lp, d = x_prompt.shape
    bs, ls, _ = x_sample.shape
    depth = w_mod.shape[0]
    assert 1 + bs <= MOD_ROWS and d == D_MODEL
    assert lp % (2 * CHUNK) == 0 and ls % (2 * CHUNK) == 0

    cond = jnp.concatenate([c_ctx[None], c, jnp.zeros((MOD_ROWS - 1 - bs, d), F32)], axis=0)
    mod = _modulation(cond, w_mod, b_mod).reshape(depth, MOD_ROWS, N_MOD, d)

    pos = _grid_pos_embed(ls, d)
    xp = x_prompt.reshape(bp * lp, d)
    xs = x_sample.reshape(bs * ls, d)
    cache = state_ssd.reshape(bs, depth, 2, SSD_GROUPS, GROUP_CH, SSD_STATE)
    fnw = final_norm_w[None]

    states = None
    for l in range(depth):
        lw = _layer_weights(l, w_in, norm1_w, gmlp_norm_w, gmlp_ws, gmlp_bs, conv_w, conv_b, dt_bias, a_log,
                            d_skip, ssd_norm_w, w_branch_a, w_branch_b, w_out, norm2_w, router_w, router_bias,
                            w_gate_e, w_up_e, w_down_e)
        final = l == depth - 1
        xp, states = _layer(xp, None, mod[l, 0:1], lw, None, states, fnw, batch=bp, seq_len=lp, layer=l,
                            emit_state=True, final=final)
        xs, _ = _layer(xs, pos if l == 0 else None, mod[l, 1:1 + bs], lw, cache, None, fnw, batch=bs, seq_len=ls,
                       layer=l, emit_state=False, final=final)

    new_state = states.reshape(bp, depth, 2, SSD_HEADS, SSD_HEADDIM, SSD_STATE)
    return (xp.reshape(bp, lp, d), xs.reshape(bs, ls, d), new_state.astype(x_prompt.dtype))
```

```python
import functools
import math

import numpy as np
import jax
import jax.numpy as jnp
from jax import lax
from jax.experimental import pallas as pl
from jax.experimental.pallas import tpu as pltpu

F32 = jnp.float32
BF16 = jnp.bfloat16

D_MODEL = 1024
CHUNK = 128
GRID_W = 64
GMLP_WIDTH = 1024
GMLP_GROUPS = 8
SSD_INNER = 2048
SSD_HEADDIM = 64
SSD_HEADS = 32
SSD_GROUPS = 4
SSD_HPG = 8
SSD_STATE = 128
SSD_CONV = 5
GROUP_CH = SSD_INNER // SSD_GROUPS
N_EXPERTS = 16
N_EXPERT_GROUPS = 4
EXPERTS_PER_GROUP = 4
D_EXPERT = 512
N_MOD = 6
EPS = 1e-6
MAIN_COLS = 2 * D_MODEL + 2 * GMLP_WIDTH + SSD_INNER + SSD_INNER + 2 * SSD_GROUPS * SSD_STATE
COL_Z = 2 * D_MODEL + 2 * GMLP_WIDTH
COL_X = COL_Z + SSD_INNER
COL_B = COL_X + SSD_INNER
COL_C = COL_B + SSD_GROUPS * SSD_STATE
DT_PAD = 128
ROW_X = 0
ROW_H = D_MODEL
ROW_AUX = 2 * D_MODEL
PAIRS_PER_GROUP = 6
N_BUCKETS = 24
ROW_W = 2 * D_MODEL + 128
AUX_MODROW = 4
MOE_TILE = 256
MIX_SUB = 128
MIX_TILE = 512
MOD_ROWS = 16

V7X_VMEM_LIMIT = 56 * 1024 * 1024


def _cparams(sem):
    return pltpu.CompilerParams(dimension_semantics=sem, vmem_limit_bytes=V7X_VMEM_LIMIT)


def _pick_tile(rows, preferred):
    tm = preferred
    while rows % tm:
        tm //= 2
    assert tm >= CHUNK
    return tm


def _split3(a):
    hi = a.astype(BF16)
    r1 = a - hi.astype(F32)
    mid = r1.astype(BF16)
    lo = (r1 - mid.astype(F32)).astype(BF16)
    return hi, mid, lo


def _split2(a):
    hi = a.astype(BF16)
    lo = (a - hi.astype(F32)).astype(BF16)
    return hi, lo


def _dot(a, b):
    return jnp.dot(a, b, preferred_element_type=F32)


def _dot_nt(a, b):
    return lax.dot_general(a, b, (((1,), (1,)), ((), ())), preferred_element_type=F32)


def _silu(x):
    return x * jax.nn.sigmoid(x)


def _gelu_tanh(x):
    c = math.sqrt(2.0 / math.pi)
    return x * (0.5 * (1.0 + jnp.tanh(c * (x + 0.044715 * (x * x * x)))))


def _softplus(x):
    return jnp.maximum(x, 0.0) + jnp.log1p(jnp.exp(-jnp.abs(x)))


def _rms(x):
    return x * lax.rsqrt(jnp.mean(x * x, axis=-1, keepdims=True) + EPS)


def _mod_kernel(cond_ref, w_ref, b_ref, o_ref):
    c = cond_ref[...]
    s_hi, s_lo = _split2(_silu(c))
    w_hi, w_lo = _split2(w_ref[0])
    o_ref[0] = _dot(s_hi, w_hi) + _dot(s_hi, w_lo) + _dot(s_lo, w_hi) + b_ref[0]


def _modulation(cond, w_mod, b_mod):
    depth, d, n = w_mod.shape
    tn = 1536
    return pl.pallas_call(
        _mod_kernel,
        grid=(depth, n // tn),
        in_specs=[pl.BlockSpec((MOD_ROWS, d), lambda l, j: (0, 0)),
                  pl.BlockSpec((1, d, tn), lambda l, j: (l, 0, j)),
                  pl.BlockSpec((1, 1, tn), lambda l, j: (l, 0, j))],
        out_specs=pl.BlockSpec((1, MOD_ROWS, tn), lambda l, j: (l, 0, j)),
        out_shape=jax.ShapeDtypeStruct((depth, MOD_ROWS, n), F32),
        compiler_params=_cparams(("arbitrary", "arbitrary")),
        name="modulation",
    )(cond, w_mod, b_mod.reshape(depth, 1, n))


def _inproj_kernel(*refs, has_pos, tm):
    if has_pos:
        x_ref, pos_ref, mod_ref, nw_ref, w_ref, wdt_ref, proj_ref, dtr_ref, h_scr = refs
    else:
        x_ref, mod_ref, nw_ref, w_ref, wdt_ref, proj_ref, dtr_ref, h_scr = refs
        pos_ref = None

    @pl.when(pl.program_id(1) == 0)
    def _():
        x = x_ref[...]
        if has_pos:
            x = x + pos_ref[...]
        h = (_rms(x) * nw_ref[...]) * (1.0 + mod_ref[0, 1:2, :]) + mod_ref[0, 0:1, :]
        hb = h.astype(BF16)
        h_scr[...] = hb
        dt = _dot(hb, wdt_ref[...])
        dtt = dt.T
        for g in range(SSD_GROUPS):
            dtr_ref[g] = dtt[g * 16:(g + 1) * 16, :]

    proj_ref[...] = _dot(h_scr[...], w_ref[...]).astype(BF16)


def _inproj(x, pos, mod, norm_w, w_main, w_dt, *, n_tokens):
    t, d = n_tokens, x.shape[1]
    tn = 1536
    nb = mod.shape[0]
    rows_per_mod = t // nb
    tm = _pick_tile(rows_per_mod, 1024)
    has_pos = pos is not None
    in_specs = [pl.BlockSpec((tm, d), lambda i, j: (i, 0))]
    args = [x]
    if has_pos:
        pos_blocks = pos.shape[0] // tm
        in_specs.append(pl.BlockSpec((tm, d), lambda i, j: (i % pos_blocks, 0)))
        args.append(pos)
    in_specs += [pl.BlockSpec((1, N_MOD, d), lambda i, j: ((i * tm) // rows_per_mod, 0, 0)),
                 pl.BlockSpec((1, d), lambda i, j: (0, 0)),
                 pl.BlockSpec((d, tn), lambda i, j: (0, j)),
                 pl.BlockSpec((d, DT_PAD), lambda i, j: (0, 0))]
    args += [mod, norm_w, w_main, w_dt]
    return pl.pallas_call(
        functools.partial(_inproj_kernel, has_pos=has_pos, tm=tm),
        grid=(t // tm, MAIN_COLS // tn),
        in_specs=in_specs,
        out_specs=[pl.BlockSpec((tm, tn), lambda i, j: (i, j)),
                   pl.BlockSpec((SSD_GROUPS, 16, tm), lambda i, j: (0, 0, i))],
        out_shape=[jax.ShapeDtypeStruct((t, MAIN_COLS), BF16),
                   jax.ShapeDtypeStruct((SSD_GROUPS, 16, t), F32)],
        scratch_shapes=[pltpu.VMEM((tm, d), BF16)],
        compiler_params=_cparams(("arbitrary", "arbitrary")),
        name="inproj",
    )(*args)


def _tri_dot_right(a, tri):
    hi, mid, lo = _split3(a)
    return _dot(hi, tri) + _dot(mid, tri) + _dot(lo, tri)


COL_ACS = (0, 16, 32)
COL_E = (48, 64)
COL_W = (80, 96)
N_PIECES = 7
ROW_Q = 0
ROW_LD = 16


def _ssd_kernel(*refs, seq_len, has_h0, emit_state, n_prev):
    it = iter(refs)
    x_ref, b_ref, c_ref, z_ref, dtr_ref = (next(it) for _ in range(5))
    cwx_ref, cwb_ref, cwc_ref, cbx_ref, cbb_ref, cbc_ref = (next(it) for _ in range(6))
    pcol_ref, dsk_ref, nw_ref, fanf_ref, fanb_ref, bcast_ref = (next(it) for _ in range(6))
    h0_ref = next(it) if has_h0 else None
    prev_ref = next(it) if n_prev else None
    y_ref = next(it)
    st_ref = next(it) if emit_state else None
    (pad_scr, xs_scr, cbf_scr, bbf_scr, btf_scr, yb_scr, yf_scr, dall_scr, pc_scr, q_scr, ld_scr,
     rt_scr, col_scr, rq_scr, hf_scr, hb_scr, xbd_scr, yd_scr) = (next(it) for _ in range(18))

    L = seq_len
    nc = L // CHUNK
    W = GROUP_CH + 2 * SSD_STATE
    n_xblk = GROUP_CH // 128

    zeros8 = jnp.zeros((8, W), F32)
    pad_scr[0:8, :] = zeros8
    pad_scr[L + 8:L + 16, :] = zeros8

    def stage(c, carry):
        r0 = pl.multiple_of(c * CHUNK, CHUNK)
        pad_scr[pl.ds(r0 + 8, CHUNK), 0:GROUP_CH] = x_ref[0, pl.ds(r0, CHUNK), :].astype(F32)
        pad_scr[pl.ds(r0 + 8, CHUNK), GROUP_CH:GROUP_CH + SSD_STATE] = b_ref[0, pl.ds(r0, CHUNK), :].astype(F32)
        pad_scr[pl.ds(r0 + 8, CHUNK), GROUP_CH + SSD_STATE:W] = c_ref[0, pl.ds(r0, CHUNK), :].astype(F32)
        return carry

    lax.fori_loop(0, nc, stage, 0)

    cw = [cwx_ref[...][:, k * 128:(k + 1) * 128] for k in range(GROUP_CH // 128)] + [cwb_ref[...], cwc_ref[...]]
    cb = [cbx_ref[...][:, k * 128:(k + 1) * 128] for k in range(GROUP_CH // 128)] + [cbb_ref[...], cbc_ref[...]]
    win_rows = CHUNK + 16

    def conv(c, carry):
        r0 = pl.multiple_of(c * CHUNK, CHUNK)
        for k in range(W // 128):
            win = pad_scr[pl.ds(r0, win_rows), k * 128:(k + 1) * 128]
            acc = win[8:8 + CHUNK, :] * cw[k][2:3, :]
            for tap in (0, 1, 3, 4):
                d = tap - 2
                rolled = pltpu.roll(win, (win_rows - d) % win_rows, 0)
                acc = acc + rolled[8:8 + CHUNK, :] * cw[k][tap:tap + 1, :]
            acc = _silu(acc + cb[k])
            if k < n_xblk:
                xs_scr[pl.ds(r0, CHUNK), k * 128:(k + 1) * 128] = acc
            elif k == n_xblk:
                bbf_scr[pl.ds(r0, CHUNK), :] = acc.astype(BF16)
                btf_scr[c] = acc.T.astype(BF16)
            else:
                cbf_scr[pl.ds(r0, CHUNK), :] = acc.astype(BF16)
        return carry

    lax.fori_loop(0, nc, conv, 0)

    ri = lax.broadcasted_iota(jnp.int32, (CHUNK, CHUNK), 0)
    ci = lax.broadcasted_iota(jnp.int32, (CHUNK, CHUNK), 1)
    below = ci < ri
    above = ci > ri
    tri_ge = jnp.where(ci >= ri, 1.0, 0.0).astype(BF16)
    tri_le = jnp.where(ci <= ri, 1.0, 0.0).astype(BF16)
    left_half = ci < SSD_HEADDIM

    bias_col = pcol_ref[0, :, 0:1]
    aneg_col = -jnp.exp(pcol_ref[0, :, 1:2])
    for c in range(nc):
        dall_scr[c * 16:(c + 1) * 16, :] = _softplus(dtr_ref[0, :, c * CHUNK:(c + 1) * CHUNK] + bias_col)
    d_all = dall_scr[...]
    nr = nc * 16
    a_all = d_all * jnp.concatenate([aneg_col] * nc, axis=0)
    fwd_rows = (lax.broadcasted_iota(jnp.int32, (nr, 1), 0) & SSD_HPG) == 0
    acs = jnp.where(fwd_rows, _tri_dot_right(a_all, tri_ge), _tri_dot_right(a_all, tri_le))
    total = jnp.where(fwd_rows, acs[:, CHUNK - 1:CHUNK], acs[:, 0:1])
    for n, piece in enumerate(_split3(acs)):
        pc_scr[n] = piece.astype(F32)
    for n, piece in enumerate(_split2(jnp.exp(acs)) + _split2(d_all * jnp.exp(total - acs))):
        pc_scr[3 + n] = piece.astype(F32)
    q_scr[...] = acs - jnp.log(d_all)
    ld_scr[...] = jnp.log(d_all + pltpu.roll(d_all, nr - SSD_HPG, 0))

    rt_scr[...] = jnp.zeros(rt_scr.shape, F32)
    for c in range(nc):
        rows = slice(c * 16, (c + 1) * 16)
        for n in range(N_PIECES):
            rt_scr[16 * n:16 * (n + 1), :] = pc_scr[n, rows, :]
        col_scr[c] = rt_scr[...].T.astype(BF16)
        rq_scr[c, ROW_Q:ROW_Q + 16, :] = q_scr[rows, :]
        rq_scr[c, ROW_LD:ROW_LD + SSD_HPG, :] = ld_scr[c * 16:c * 16 + SSD_HPG, :]

    def fans(col, fan_ref):
        both = _dot(col, fan_ref[...])
        return both[:, 0:GROUP_CH], both[:, GROUP_CH:2 * GROUP_CH]

    if has_h0:
        hf_scr[...] = h0_ref[0, 0, 0, 0].T
        hb_scr[...] = h0_ref[0, 0, 1, 0].T
    else:
        hf_scr[...] = jnp.zeros((SSD_STATE, GROUP_CH), F32)
        hb_scr[...] = jnp.zeros((SSD_STATE, GROUP_CH), F32)
    xbd_scr[...] = jnp.zeros(xbd_scr.shape, BF16)

    n_half = GROUP_CH // 256

    def sweeps(i, carry):
        passes = ((nc - 1 - i, hb_scr, fanb_ref, yb_scr, 0), (i, hf_scr, fanf_ref, yf_scr, CHUNK - 1))
        staged = []
        for c, h_scr, fan_ref, y_scr, total_row in passes:
            r0 = pl.multiple_of(c * CHUNK, CHUNK)
            e_x, w_x = fans(col_scr[c], fan_ref)
            staged.append((r0, cbf_scr[pl.ds(r0, CHUNK), :], btf_scr[c], e_x, w_x))
        for hf in range(n_half):
            sl = slice(hf * 256, (hf + 1) * 256)
            for (c, h_scr, fan_ref, y_scr, total_row), (r0, cm, bt, e_x, w_x) in zip(passes, staged):
                e2 = e_x[:, sl]
                h = h_scr[:, sl]
                y_scr[pl.ds(r0, CHUNK), sl] = _dot(cm, h.astype(BF16)) * e2
                xw = (xs_scr[pl.ds(r0, CHUNK), sl] * w_x[:, sl]).astype(BF16)
                h_scr[:, sl] = h * e2[total_row:total_row + 1, :] + _dot(bt, xw)
        return carry

    lax.fori_loop(0, nc, sweeps, 0)

    def finish(j, carry):
        work = []
        for u in range(2):
            c = 2 * j + u
            r0 = pl.multiple_of(c * CHUNK, CHUNK)
            cb_mat = _dot_nt(cbf_scr[pl.ds(r0, CHUNK), :], bbf_scr[pl.ds(r0, CHUNK), :])
            acs_b = _dot(col_scr[c], bcast_ref[...])
            work.append((u, r0, acs_b, rq_scr[c], cb_mat))

        for u, r0, col, rq, cb_mat in work:
            for p in range(n_xblk):
                xb = xs_scr[pl.ds(r0, CHUNK), p * 128:(p + 1) * 128].astype(BF16)
                xbd_scr[u, p, 0:CHUNK, 0:SSD_HEADDIM] = xb[:, 0:SSD_HEADDIM]
                xbd_scr[u, p, CHUNK:2 * CHUNK, SSD_HEADDIM:2 * SSD_HEADDIM] = xb[:, SSD_HEADDIM:2 * SSD_HEADDIM]
        for p in range(n_xblk):
            for u, r0, col, rq, cb_mat in work:
                wpair = []
                for hh in (2 * p, 2 * p + 1):
                    pf = col[:, hh * CHUNK:(hh + 1) * CHUNK]
                    pb = col[:, (SSD_HPG + hh) * CHUNK:(SSD_HPG + hh + 1) * CHUNK]
                    qf = rq[ROW_Q + hh:ROW_Q + hh + 1, :]
                    qb = rq[ROW_Q + SSD_HPG + hh:ROW_Q + SSD_HPG + hh + 1, :]
                    ld = rq[ROW_LD + hh:ROW_LD + hh + 1, :]
                    arg = jnp.where(below, pf - qf, jnp.where(above, pb - qb, ld))
                    wpair.append((cb_mat * jnp.exp(arg)).astype(BF16))
                yd_scr[u, :, p * 128:(p + 1) * 128] = _dot(jnp.concatenate(wpair, axis=1), xbd_scr[u, p])

        for u, r0, col, rq, cb_mat in work:
            ssq = jnp.zeros((CHUNK, 1), F32)
            for hf in range(n_half):
                sl = slice(hf * 256, (hf + 1) * 256)
                y = (yb_scr[pl.ds(r0, CHUNK), sl] + yf_scr[pl.ds(r0, CHUNK), sl] + yd_scr[u, :, sl]
                     + xs_scr[pl.ds(r0, CHUNK), sl] * dsk_ref[:, sl])
                yg = y * _silu(z_ref[0, pl.ds(r0, CHUNK), sl].astype(F32))
                yd_scr[u, :, sl] = yg
                ssq = ssq + jnp.sum(yg * yg, axis=1, keepdims=True)
            scale = lax.rsqrt(ssq * (1.0 / GROUP_CH) + EPS)
            for hf in range(n_half):
                sl = slice(hf * 256, (hf + 1) * 256)
                y_ref[0, pl.ds(r0, CHUNK), sl] = (yd_scr[u, :, sl] * scale * nw_ref[:, sl]).astype(BF16)
        return carry

    lax.fori_loop(0, nc // 2, finish, 0)

    if emit_state:
        for l in range(n_prev):
            st_ref[0, l] = prev_ref[0, l]
        st_ref[0, n_prev, 0, 0] = hf_scr[...].T
        st_ref[0, n_prev, 1, 0] = hb_scr[...].T


def _fan_matrix(first_head):
    m = np.zeros((CHUNK, 2 * GROUP_CH), np.float32)
    for part, bases in enumerate((COL_E, COL_W)):
        for base in bases:
            for j in range(SSD_HPG):
                m[base + first_head + j,
                  part * GROUP_CH + j * SSD_HEADDIM:part * GROUP_CH + (j + 1) * SSD_HEADDIM] = 1.0
    return jnp.asarray(m, BF16)


def _bcast_matrix():
    m = np.zeros((CHUNK, 2 * SSD_HPG * CHUNK), np.float32)
    for base in COL_ACS:
        for j in range(2 * SSD_HPG):
            m[base + j, j * CHUNK:(j + 1) * CHUNK] = 1.0
    return jnp.asarray(m, BF16)


def _ssd(proj, dtr, lw, h0, prev_state, *, batch, seq_len, layer, emit_state):
    L = seq_len
    proj3 = proj.reshape(batch, L, MAIN_COLS)
    has_h0 = h0 is not None
    gc = GROUP_CH
    W = gc + 2 * SSD_STATE

    in_specs = [
        pl.BlockSpec((1, L, gc), lambda b, g: (b, 0, COL_X // gc + g)),
        pl.BlockSpec((1, L, SSD_STATE), lambda b, g: (b, 0, COL_B // SSD_STATE + g)),
        pl.BlockSpec((1, L, SSD_STATE), lambda b, g: (b, 0, COL_C // SSD_STATE + g)),
        pl.BlockSpec((1, L, gc), lambda b, g: (b, 0, COL_Z // gc + g)),
        pl.BlockSpec((1, 16, L), lambda b, g: (g, 0, b)),
        pl.BlockSpec((SSD_CONV, gc), lambda b, g: (0, g)),
        pl.BlockSpec((SSD_CONV, SSD_STATE), lambda b, g: (0, SSD_INNER // SSD_STATE + g)),
        pl.BlockSpec((SSD_CONV, SSD_STATE), lambda b, g: (0, SSD_INNER // SSD_STATE + SSD_GROUPS + g)),
        pl.BlockSpec((1, gc), lambda b, g: (0, g)),
        pl.BlockSpec((1, SSD_STATE), lambda b, g: (0, SSD_INNER // SSD_STATE + g)),
        pl.BlockSpec((1, SSD_STATE), lambda b, g: (0, SSD_INNER // SSD_STATE + SSD_GROUPS + g)),
        pl.BlockSpec((1, 16, 2), lambda b, g: (g, 0, 0)),
        pl.BlockSpec((1, gc), lambda b, g: (0, g)),
        pl.BlockSpec((1, gc), lambda b, g: (0, g)),
        pl.BlockSpec((CHUNK, 2 * gc), lambda b, g: (0, 0)),
        pl.BlockSpec((CHUNK, 2 * gc), lambda b, g: (0, 0)),
        pl.BlockSpec((CHUNK, 2 * SSD_HPG * CHUNK), lambda b, g: (0, 0)),
    ]
    args = [proj3, proj3, proj3, proj3, dtr,
            lw["conv_w"], lw["conv_w"], lw["conv_w"], lw["conv_b"], lw["conv_b"], lw["conv_b"],
            lw["ssd_pcol"], lw["d_skip_x"], lw["ssd_norm_w"], _fan_matrix(0), _fan_matrix(SSD_HPG),
            _bcast_matrix()]
    if has_h0:
        in_specs.append(pl.BlockSpec((1, 1, 2, 1, gc, SSD_STATE), lambda b, g: (b, layer, 0, g, 0, 0)))
        args.append(h0)
    out_specs = [pl.BlockSpec((1, L, gc), lambda b, g: (b, 0, g))]
    out_shape = [jax.ShapeDtypeStruct((batch, L, SSD_INNER), BF16)]
    n_prev = 0 if prev_state is None else prev_state.shape[1]
    if n_prev:
        in_specs.append(pl.BlockSpec((1, n_prev, 2, 1, gc, SSD_STATE), lambda b, g: (b, 0, 0, g, 0, 0)))
        args.append(prev_state)
    if emit_state:
        out_specs.append(pl.BlockSpec((1, n_prev + 1, 2, 1, gc, SSD_STATE), lambda b, g: (b, 0, 0, g, 0, 0)))
        out_shape.append(jax.ShapeDtypeStruct((batch, n_prev + 1, 2, SSD_GROUPS, gc, SSD_STATE), F32))
    nc = L // CHUNK
    scratch = [
        pltpu.VMEM((L + 16, W), F32),
        pltpu.VMEM((L, gc), F32),
        pltpu.VMEM((L, SSD_STATE), BF16),
        pltpu.VMEM((L, SSD_STATE), BF16),
        pltpu.VMEM((nc, SSD_STATE, CHUNK), BF16),
        pltpu.VMEM((L, gc), F32),
        pltpu.VMEM((L, gc), F32),
        pltpu.VMEM((nc * 16, CHUNK), F32),
        pltpu.VMEM((N_PIECES, nc * 16, CHUNK), F32),
        pltpu.VMEM((nc * 16, CHUNK), F32),
        pltpu.VMEM((nc * 16, CHUNK), F32),
        pltpu.VMEM((CHUNK, CHUNK), F32),
        pltpu.VMEM((nc, CHUNK, CHUNK), BF16),
        pltpu.VMEM((nc, 24, CHUNK), F32),
        pltpu.VMEM((SSD_STATE, gc), F32),
        pltpu.VMEM((SSD_STATE, gc), F32),
        pltpu.VMEM((2, gc // 128, 2 * CHUNK, 128), BF16),
        pltpu.VMEM((2, CHUNK, gc), F32),
    ]
    outs = pl.pallas_call(
        functools.partial(_ssd_kernel, seq_len=L, has_h0=has_h0, emit_state=emit_state, n_prev=n_prev),
        grid=(batch, SSD_GROUPS),
        in_specs=in_specs,
        out_specs=out_specs,
        out_shape=out_shape,
        scratch_shapes=scratch,
        compiler_params=_cparams(("arbitrary", "arbitrary")),
        name="ssd",
    )(*args)
    y = outs[0].reshape(batch * L, SSD_INNER)
    return y, (outs[1] if emit_state else None)


def _route(sel, scores):
    neg = -jnp.inf

    def first_max(vals):
        m = vals[0]
        for v in vals[1:]:
            m = jnp.maximum(m, v)
        taken = jnp.zeros_like(m)
        flags = []
        for v in vals:
            f = jnp.where(v == m, 1.0, 0.0) * (1.0 - taken)
            flags.append(f)
            taken = taken + f
        return m, flags

    group_scores = []
    for j in range(N_EXPERT_GROUPS):
        a = sel[j * EXPERTS_PER_GROUP:(j + 1) * EXPERTS_PER_GROUP]
        m1, f1 = first_max(a)
        m2, _ = first_max([jnp.where(f > 0.5, neg, v) for f, v in zip(f1, a)])
        group_scores.append(m1 + m2)
    _, gflag = first_max(group_scores)
    masked = [jnp.where(gflag[e // EXPERTS_PER_GROUP] > 0.5, sel[e], neg) for e in range(N_EXPERTS)]
    _, f1 = first_max(masked)
    _, f2 = first_max([jnp.where(f > 0.5, neg, v) for f, v in zip(f1, masked)])
    w1 = sum(f * s for f, s in zip(f1, scores))
    w2 = sum(f * s for f, s in zip(f2, scores))
    tot = w1 + w2
    gates = [(f1[e] * w1 + f2[e] * w2) / tot for e in range(N_EXPERTS)]
    taken = jnp.zeros_like(tot)
    e_lo = e_hi = g_lo = g_hi = jnp.zeros_like(tot)
    for e in range(N_EXPERTS):
        sel_e = f1[e] + f2[e]
        low = sel_e * (1.0 - taken)
        high = sel_e - low
        taken = taken + sel_e
        e_lo = e_lo + float(e) * low
        e_hi = e_hi + float(e) * high
        g_lo = g_lo + low * gates[e]
        g_hi = g_hi + high * gates[e]
    group = sum(float(j) * gflag[j] for j in range(1, N_EXPERT_GROUPS))
    a = e_lo - EXPERTS_PER_GROUP * group
    b = e_hi - EXPERTS_PER_GROUP * group
    pair = a * (2 * EXPERTS_PER_GROUP - 1 - a) * 0.5 + (b - a - 1.0)
    bucket = group * PAIRS_PER_GROUP + pair
    return (g_lo, g_hi), bucket


def _mixout_kernel(*refs, has_pos, tm, rows_per_mod):
    it = iter(refs)
    x_ref = next(it)
    pos_ref = next(it) if has_pos else None
    ga_ref, gb_ref, u_ref, v_ref, yb_ref, mod_ref = (next(it) for _ in range(6))
    gnw_ref, ws_ref, bsx_ref, wa_ref, wb_ref, wo_ref, n2w_ref, rwt_ref, rb_ref = (next(it) for _ in range(9))
    rows_ref, gid_ref = (next(it) for _ in range(2))
    g_scr = next(it)

    sub = min(tm, MIX_SUB)
    spans = [slice(s * sub, (s + 1) * sub) for s in range(tm // sub)]
    n_chunks = tm // CHUNK
    gd = GMLP_WIDTH // GMLP_GROUPS

    u = [_gelu_tanh(u_ref[r, :].astype(F32)) for r in spans]
    vn = [(_rms(_gelu_tanh(v_ref[r, :].astype(F32))) * gnw_ref[...]).astype(BF16) for r in spans]

    def chunk_rows(c):
        s, local = divmod(c * CHUNK, sub)
        return s, slice(local, local + CHUNK)

    mixed_cols = [[None] * GMLP_GROUPS for _ in range(n_chunks)]
    for g in range(GMLP_GROUPS):
        pieces = []
        for c in range(n_chunks):
            s, rr = chunk_rows(c)
            pieces.append(vn[s][rr, g * gd:(g + 1) * gd])
        res = _dot(ws_ref[g], jnp.concatenate(pieces, axis=1))
        for c in range(n_chunks):
            mixed_cols[c][g] = res[:, c * gd:(c + 1) * gd]
    per_sub = sub // CHUNK
    ya = []
    for s in range(len(spans)):
        mixed = jnp.concatenate([jnp.concatenate(mixed_cols[s * per_sub + c], axis=1) + bsx_ref[...]
                                 for c in range(per_sub)], axis=0)
        ya.append((u[s] * mixed).astype(BF16))

    da = [_dot(y, wa_ref[...]) for y in ya]
    db = [_dot(yb_ref[r, :], wb_ref[...]) for r in spans]
    merged = [(jax.nn.sigmoid(ga_ref[r, :].astype(F32)) * a + jax.nn.sigmoid(gb_ref[r, :].astype(F32)) * b
               ).astype(BF16) for r, a, b in zip(spans, da, db)]
    do = [_dot(m, wo_ref[...]) for m in merged]

    h2s = []
    for r, o in zip(spans, do):
        x = x_ref[r, :]
        if has_pos:
            x = x + pos_ref[r, :]
        xn = x + mod_ref[0, 2:3, :] * o
        rows_ref[r, ROW_X:ROW_X + D_MODEL] = xn
        h2 = (_rms(xn) * n2w_ref[...]) * (1.0 + mod_ref[0, 4:5, :]) + mod_ref[0, 3:4, :]
        rows_ref[r, ROW_H:ROW_H + D_MODEL] = h2
        h2s.append(h2)

    r_hi, r_lo = _split2(rwt_ref[...])
    g_scr[...] = jnp.zeros(g_scr.shape, F32)
    mod_row = (pl.program_id(0) * tm) // rows_per_mod
    g_scr[AUX_MODROW:AUX_MODROW + 1, :] = jnp.full((1, tm), mod_row, jnp.int32).astype(F32)
    for r, h2 in zip(spans, h2s):
        h_hi, h_lo = _split2(h2)
        logits = _dot_nt(r_hi, h_hi) + _dot_nt(r_hi, h_lo) + _dot_nt(r_lo, h_hi)
        scores = jax.nn.sigmoid(logits)
        selm = scores + rb_ref[...]
        pair_gates, bucket = _route([selm[e:e + 1, :] for e in range(N_EXPERTS)],
                                    [scores[e:e + 1, :] for e in range(N_EXPERTS)])
        gid_ref[:, r] = bucket.astype(jnp.int32)
        for k in range(2):
            g_scr[k:k + 1, r] = pair_gates[k]
    rows_ref[:, ROW_AUX:ROW_W] = g_scr[...].T


def _mixout(x, pos, proj, yb, mod, lw, *, n_tokens):
    t, d = n_tokens, x.shape[1]
    nb = mod.shape[0]
    rows_per_mod = t // nb
    tm = _pick_tile(rows_per_mod, MIX_TILE)
    has_pos = pos is not None
    full = lambda shape: pl.BlockSpec(shape, lambda i: (0,) * len(shape), pipeline_mode=pl.Buffered(1))
    in_specs = [pl.BlockSpec((tm, d), lambda i: (i, 0))]
    args = [x]
    if has_pos:
        pos_blocks = pos.shape[0] // tm
        in_specs.append(pl.BlockSpec((tm, d), lambda i: (i % pos_blocks, 0)))
        args.append(pos)
    in_specs += [pl.BlockSpec((tm, d), lambda i: (i, 0)),
                 pl.BlockSpec((tm, d), lambda i: (i, 1)),
                 pl.BlockSpec((tm, d), lambda i: (i, 2)),
                 pl.BlockSpec((tm, d), lambda i: (i, 3)),
                 pl.BlockSpec((tm, SSD_INNER), lambda i: (i, 0)),
                 pl.BlockSpec((1, N_MOD, d), lambda i: ((i * tm) // rows_per_mod, 0, 0)),
                 full((1, GMLP_WIDTH)),
                 full((GMLP_GROUPS, CHUNK, CHUNK)),
                 full((CHUNK, GMLP_WIDTH)),
                 full((GMLP_WIDTH, d)),
                 full((SSD_INNER, d)),
                 full((d, d)),
                 full((1, d)),
                 full((N_EXPERTS, d)),
                 full((N_EXPERTS, 1))]
    args += [proj, proj, proj, proj, yb, mod,
             lw["gmlp_norm_w"], lw["gmlp_ws"], lw["gmlp_bs_x"], lw["w_branch_a"], lw["w_branch_b"], lw["w_out"],
             lw["norm2_w"], lw["router_wt"], lw["router_bias"]]
    return pl.pallas_call(
        functools.partial(_mixout_kernel, has_pos=has_pos, tm=tm, rows_per_mod=rows_per_mod),
        grid=(t // tm,),
        in_specs=in_specs,
        out_specs=[pl.BlockSpec((tm, ROW_W), lambda i: (i, 0)),
                   pl.BlockSpec((1, tm), lambda i: (0, i))],
        out_shape=[jax.ShapeDtypeStruct((t, ROW_W), F32),
                   jax.ShapeDtypeStruct((1, t), jnp.int32)],
        scratch_shapes=[pltpu.VMEM((ROW_W - ROW_AUX, tm), F32)],
        compiler_params=_cparams(("arbitrary",)),
        name="mixout",
    )(*args)


def _moe_plan(gid, n_tokens, tm):
    nb = N_BUCKETS
    t = n_tokens
    i32 = jnp.int32
    onehot = (gid[:, None] == jnp.arange(nb, dtype=i32)[None, :]).astype(i32)
    csum = jnp.cumsum(onehot, axis=0)
    rank = jnp.sum(onehot * (csum - 1), axis=1)
    counts = csum[-1]
    padded = ((counts + tm - 1) // tm) * tm
    ends = jnp.cumsum(padded)
    dest = jnp.sum(onehot * (ends - padded)[None, :], axis=1) + rank
    nt = pl.cdiv(t, tm) + nb
    tile_bucket = jnp.sum((jnp.arange(nt, dtype=i32) * tm)[:, None] >= ends[None, :], axis=1)
    tile_bucket = jnp.minimum(tile_bucket, nb - 1)
    pairs = [(a, b) for a in range(EXPERTS_PER_GROUP) for b in range(a + 1, EXPERTS_PER_GROUP)]
    lo = jnp.asarray([g * EXPERTS_PER_GROUP + a for g in range(N_EXPERT_GROUPS) for a, _ in pairs], i32)
    hi = jnp.asarray([g * EXPERTS_PER_GROUP + b for g in range(N_EXPERT_GROUPS) for _, b in pairs], i32)
    n_used = (ends[-1] // tm).astype(i32)
    tail = n_used + jnp.arange(nb, dtype=i32)
    cand = jnp.concatenate([jnp.where(padded > 0, ends - tm, -1), jnp.where(tail < nt, tail * tm, -1)])
    keep = cand >= 0
    order = jnp.argsort(jnp.logical_not(keep), stable=True)
    fill = jnp.concatenate([jnp.sum(keep).reshape(1), cand[order]]).astype(i32)
    return dest.astype(i32), lo[tile_bucket], hi[tile_bucket], n_used.reshape(1), fill


def _permute_kernel(*refs, tn, scatter, fill_rows):
    if fill_rows:
        fill_ref, idx_ref, src_ref, out_ref, zbuf, sem, zsem = refs
    else:
        idx_ref, src_ref, out_ref, sem = refs
    out_hbm = out_ref
    i = pl.program_id(0)

    if fill_rows:
        @pl.when(i == 0)
        def _():
            zbuf[...] = jnp.zeros(zbuf.shape, F32)
            n_fill = fill_ref[0]

            def start_fill(k, carry):
                start = pl.multiple_of(fill_ref[1 + k], fill_rows)
                pltpu.make_async_copy(zbuf, out_hbm.at[pl.ds(start, fill_rows), :], zsem).start()
                return carry

            def wait_fill(k, carry):
                pltpu.make_async_copy(zbuf, out_hbm.at[pl.ds(0, fill_rows), :], zsem).wait()
                return carry

            lax.fori_loop(0, n_fill, start_fill, 0)
            lax.fori_loop(0, n_fill, wait_fill, 0)

    group = 16

    def body(j, carry):
        r0 = pl.multiple_of(j * group, group)
        for k in range(group):
            p = idx_ref[0, 0, r0 + k]
            t8, s8 = j * (group // 8) + k // 8, k % 8
            if scatter:
                cp = pltpu.make_async_copy(src_ref.at[t8, pl.ds(s8, 1), :], out_ref.at[pl.ds(p, 1), :], sem)
            else:
                cp = pltpu.make_async_copy(src_ref.at[pl.ds(p, 1), :], out_ref.at[t8, pl.ds(s8, 1), :], sem)
            cp.start()
        return carry

    lax.fori_loop(0, tn // group, body, 0)
    block_ref = src_ref if scatter else out_ref
    pltpu.make_async_copy(block_ref, block_ref, sem).wait()


def _permute_rows(src, idx, *, n_rows, n_out, scatter, fill=None, fill_rows=0):
    w = src.shape[1]
    tn = _pick_tile(n_rows, 1024)
    steps = n_rows // tn
    idx3 = idx.reshape(steps, 1, tn)
    block = pl.BlockSpec((tn // 8, 8, w), lambda i, *_: (i, 0, 0))
    hbm = pl.BlockSpec(memory_space=pl.ANY)
    if scatter:
        src = src.reshape(src.shape[0] // 8, 8, w)
    in_specs = [pl.BlockSpec((1, 1, tn), lambda i, *_: (i, 0, 0), memory_space=pltpu.SMEM),
                block if scatter else hbm]
    scratch = [pltpu.SemaphoreType.DMA(())]
    args = [idx3, src]
    n_prefetch = 0
    if fill_rows:
        n_prefetch = 1
        args = [fill] + args
        scratch = [pltpu.VMEM((fill_rows, w), F32), pltpu.SemaphoreType.DMA(()), pltpu.SemaphoreType.DMA(())]
    out = pl.pallas_call(
        functools.partial(_permute_kernel, tn=tn, scatter=scatter, fill_rows=fill_rows),
        grid_spec=pltpu.PrefetchScalarGridSpec(
            num_scalar_prefetch=n_prefetch, grid=(steps,), in_specs=in_specs,
            out_specs=hbm if scatter else block, scratch_shapes=scratch),
        out_shape=jax.ShapeDtypeStruct((n_out, w) if scatter else (n_out // 8, 8, w), F32),
        compiler_params=_cparams(("arbitrary",)),
        name="permute_scatter" if scatter else "permute_gather",
    )(*args)
    return out if scatter else out.reshape(n_out, w)


def _moe_kernel(elo_ref, ehi_ref, nused_ref, xs_ref, g2_ref, wg0_ref, wu0_ref, wd0_ref, wg1_ref, wu1_ref, wd1_ref,
                fnw_ref, o_ref, *, tm, final, n_mod):
    used = pl.program_id(0) < nused_ref[0]

    @pl.when(jnp.logical_not(used))
    def _():
        o_ref[...] = jnp.zeros(o_ref.shape, F32)

    @pl.when(used)
    def _():
        rows = xs_ref[...]
        h = rows[:, ROW_H:ROW_H + D_MODEL].astype(BF16)
        y = None
        for k, (wg_ref, wu_ref, wd_ref) in enumerate(((wg0_ref, wu0_ref, wd0_ref), (wg1_ref, wu1_ref, wd1_ref))):
            act = _silu(_dot(h, wg_ref[0])) * _dot(h, wu_ref[0])
            gate = rows[:, ROW_AUX + k:ROW_AUX + k + 1]
            part = _dot((act * gate).astype(BF16), wd_ref[0])
            y = part if y is None else y + part
        if n_mod == 1:
            g2 = g2_ref[0:1, :]
        else:
            n = g2_ref.shape[0]
            mod_row = rows[:, ROW_AUX + AUX_MODROW:ROW_AUX + AUX_MODROW + 1]
            ids = lax.broadcasted_iota(jnp.int32, (tm, 2 * n), 1)
            ids = jnp.where(ids >= n, ids - n, ids).astype(F32)
            onehot = jnp.where(mod_row == ids, 1.0, 0.0).astype(BF16)
            g2 = _dot(onehot, jnp.concatenate(_split2(g2_ref[...]), axis=0))
        xn = rows[:, ROW_X:ROW_X + D_MODEL] + g2 * y
        if final:
            xn = _rms(xn) * fnw_ref[...]
        o_ref[...] = xn


def _moe(rows, gid, g2, lw, final_norm_w, *, n_tokens, final):
    t, d, tm = n_tokens, D_MODEL, MOE_TILE
    dest, tile_lo, tile_hi, n_used, fill = _moe_plan(gid.reshape(t), t, tm)
    nt = tile_lo.shape[0]
    n_sorted = nt * tm
    n_mod = g2.shape[0]
    if n_mod > 1:
        g2 = jnp.pad(g2, ((0, (-n_mod) % 8), (0, 0)))
    xs = _permute_rows(rows, dest, n_rows=t, n_out=n_sorted, scatter=True, fill=fill, fill_rows=tm)

    def tile(i, nu):
        return jnp.minimum(i, nu[0] - 1)

    def expert_specs(which):
        def idx(i, elo, ehi, nu):
            return ((elo, ehi)[which][tile(i, nu)], 0, 0)
        return [pl.BlockSpec((1, d, D_EXPERT), idx), pl.BlockSpec((1, d, D_EXPERT), idx),
                pl.BlockSpec((1, D_EXPERT, d), idx)]

    grid_spec = pltpu.PrefetchScalarGridSpec(
        num_scalar_prefetch=3,
        grid=(nt,),
        in_specs=[pl.BlockSpec((tm, ROW_W), lambda i, elo, ehi, nu: (tile(i, nu), 0)),
                  pl.BlockSpec(g2.shape, lambda i, elo, ehi, nu: (0, 0))]
                 + expert_specs(0) + expert_specs(1)
                 + [pl.BlockSpec((1, d), lambda i, elo, ehi, nu: (0, 0))],
        out_specs=pl.BlockSpec((tm, d), lambda i, elo, ehi, nu: (i, 0)))
    we = (lw["w_gate_e"], lw["w_up_e"], lw["w_down_e"])
    ys = pl.pallas_call(
        functools.partial(_moe_kernel, tm=tm, final=final, n_mod=n_mod),
        grid_spec=grid_spec,
        out_shape=jax.ShapeDtypeStruct((n_sorted, d), F32),
        compiler_params=_cparams(("arbitrary",)),
        name="moe",
    )(tile_lo, tile_hi, n_used, xs, g2, *we, *we, final_norm_w)
    return _permute_rows(ys, dest, n_rows=t, n_out=t, scatter=False)


def _grid_pos_embed(n_tokens, dim):
    rows = n_tokens // GRID_W
    quarter = dim // 4
    omega = 1.0 / (10000.0 ** (jnp.arange(quarter, dtype=F32) / quarter))
    r = jnp.arange(rows, dtype=F32)[:, None] * omega
    col = jnp.arange(GRID_W, dtype=F32)[:, None] * omega
    r_emb = jnp.concatenate([jnp.sin(r), jnp.cos(r)], axis=-1)
    c_emb = jnp.concatenate([jnp.sin(col), jnp.cos(col)], axis=-1)
    emb = jnp.concatenate([
        jnp.broadcast_to(r_emb[:, None, :], (rows, GRID_W, dim // 2)),
        jnp.broadcast_to(c_emb[None, :, :], (rows, GRID_W, dim // 2))], axis=-1)
    return emb.reshape(rows * GRID_W, dim)


def _dt_perm():
    return np.array([dr * SSD_HEADS + g * SSD_HPG + j
                     for g in range(SSD_GROUPS) for dr in range(2) for j in range(SSD_HPG)], np.int32)


def _layer_weights(l, w_in, norm1_w, gmlp_norm_w, gmlp_ws, gmlp_bs, conv_w, conv_b, dt_bias, a_log, d_skip,
                   ssd_norm_w, w_branch_a, w_branch_b, w_out, norm2_w, router_w, router_bias,
                   w_gate_e, w_up_e, w_down_e):
    perm = _dt_perm()
    w_dt = jnp.pad(w_in[l][:, MAIN_COLS:][:, perm], ((0, 0), (0, DT_PAD - 2 * SSD_HEADS)))
    prow = jnp.stack([dt_bias[l].reshape(-1)[perm], a_log[l].reshape(-1)[perm]], axis=0)
    prow = prow.reshape(2, SSD_GROUPS, 16).transpose(1, 0, 2)
    return {
        "norm1_w": norm1_w[l][None],
        "w_main": w_in[l][:, :MAIN_COLS].astype(BF16),
        "w_dt": w_dt.astype(BF16),
        "gmlp_norm_w": gmlp_norm_w[l][None],
        "gmlp_ws": gmlp_ws[l].astype(BF16),
        "gmlp_bs_x": jnp.repeat(gmlp_bs[l].T, GMLP_WIDTH // GMLP_GROUPS, axis=1),
        "conv_w": conv_w[l],
        "conv_b": conv_b[l][None],
        "ssd_pcol": prow.transpose(0, 2, 1),
        "d_skip_x": jnp.repeat(d_skip[l], SSD_HEADDIM)[None],
        "ssd_norm_w": ssd_norm_w[l][None],
        "w_branch_a": w_branch_a[l].astype(BF16),
        "w_branch_b": w_branch_b[l].astype(BF16),
        "w_out": w_out[l].astype(BF16),
        "norm2_w": norm2_w[l][None],
        "router_wt": router_w.T,
        "router_bias": router_bias[:, None],
        "w_gate_e": w_gate_e[l].astype(BF16),
        "w_up_e": w_up_e[l].astype(BF16),
        "w_down_e": w_down_e[l].astype(BF16),
    }


def _layer(x, pos, mod, lw, h0, prev_state, final_norm_w, *, batch, seq_len, layer, emit_state, final):
    t = batch * seq_len
    proj, dtr = _inproj(x, pos, mod, lw["norm1_w"], lw["w_main"], lw["w_dt"], n_tokens=t)
    yb, state = _ssd(proj, dtr, lw, h0, prev_state, batch=batch, seq_len=seq_len, layer=layer,
                     emit_state=emit_state)
    rows, gid = _mixout(x, pos, proj, yb, mod, lw, n_tokens=t)
    out = _moe(rows, gid, mod[:, N_MOD - 1, :], lw, final_norm_w, n_tokens=t, final=final)
    return out, state


def kernel(x_prompt, x_sample, state_ssd, c, c_ctx, w_mod, b_mod, norm1_w, w_in, gmlp_norm_w, gmlp_ws, gmlp_bs, conv_w, conv_b, dt_bias, a_log, d_skip, ssd_norm_w, w_branch_a, w_branch_b, w_out, norm2_w, router_w, router_bias, w_gate_e, w_up_e, w_down_e, final_norm_w):
    bp, lp, d = x_prompt.shape
    bs, ls, _ = x_sample.shape
    depth = w_mod.shape[0]
    assert 1 + bs <= MOD_ROWS and d == D_MODEL
    assert lp % (2 * CHUNK) == 0 and ls % (2 * CHUNK) == 0

    cond = jnp.concatenate([c_ctx[None], c, jnp.zeros((MOD_ROWS - 1 - bs, d), F32)], axis=0)
    mod = _modulation(cond, w_mod, b_mod).reshape(depth, MOD_ROWS, N_MOD, d)

    pos = _grid_pos_embed(ls, d)
    xp = x_prompt.reshape(bp * lp, d)
    xs = x_sample.reshape(bs * ls, d)
    cache = state_ssd.reshape(bs, depth, 2, SSD_GROUPS, GROUP_CH, SSD_STATE)
    fnw = final_norm_w[None]

    states = None
    for l in range(depth):
        lw = _layer_weights(l, w_in, norm1_w, gmlp_norm_w, gmlp_ws, gmlp_bs, conv_w, conv_b, dt_bias, a_log,
                            d_skip, ssd_norm_w, w_branch_a, w_branch_b, w_out, norm2_w, router_w, router_bias,
                            w_gate_e, w_up_e, w_down_e)
        final = l == depth - 1
        xp, states = _layer(xp, None, mod[l, 0:1], lw, None, states, fnw, batch=bp, seq_len=lp, layer=l,
                            emit_state=True, final=final)
        xs, _ = _layer(xs, pos if l == 0 else None, mod[l, 1:1 + bs], lw, cache, None, fnw, batch=bs, seq_len=ls,
                       layer=l, emit_state=False, final=final)

    new_state = states.reshape(bp, depth, 2, SSD_HEADS, SSD_HEADDIM, SSD_STATE)
    return (xp.reshape(bp, lp, d), xs.reshape(bs, ls, d), new_state.astype(x_prompt.dtype))
```

```python
import functools
import math

import numpy as np
import jax
import jax.numpy as jnp
from jax import lax
from jax.experimental import pallas as pl
from jax.experimental.pallas import tpu as pltpu

F32 = jnp.float32
BF16 = jnp.bfloat16

D_MODEL = 1024
CHUNK = 128
GRID_W = 64
GMLP_WIDTH = 1024
GMLP_GROUPS = 8
SSD_INNER = 2048
SSD_HEADDIM = 64
SSD_HEADS = 32
SSD_GROUPS = 4
SSD_HPG = 8
SSD_STATE = 128
SSD_CONV = 5
GROUP_CH = SSD_INNER // SSD_GROUPS
N_EXPERTS = 16
N_EXPERT_GROUPS = 4
EXPERTS_PER_GROUP = 4
D_EXPERT = 512
N_MOD = 6
EPS = 1e-6
MAIN_COLS = 2 * D_MODEL + 2 * GMLP_WIDTH + SSD_INNER + SSD_INNER + 2 * SSD_GROUPS * SSD_STATE
COL_Z = 2 * D_MODEL + 2 * GMLP_WIDTH
COL_X = COL_Z + SSD_INNER
COL_B = COL_X + SSD_INNER
COL_C = COL_B + SSD_GROUPS * SSD_STATE
DT_PAD = 128
ROW_X = 0
ROW_H = D_MODEL
ROW_AUX = 2 * D_MODEL
PAIRS_PER_GROUP = 6
N_BUCKETS = 24
ROW_W = 2 * D_MODEL + 128
AUX_MODROW = 4
MOE_TILE = 256
MIX_SUB = 128
MIX_TILE = 512
MOD_ROWS = 16

V7X_VMEM_LIMIT = 56 * 1024 * 1024


def _cparams(sem):
    return pltpu.CompilerParams(dimension_semantics=sem, vmem_limit_bytes=V7X_VMEM_LIMIT)


def _pick_tile(rows, preferred):
    tm = preferred
    while rows % tm:
        tm //= 2
    assert tm >= CHUNK
    return tm


def _split3(a):
    hi = a.astype(BF16)
    r1 = a - hi.astype(F32)
    mid = r1.astype(BF16)
    lo = (r1 - mid.astype(F32)).astype(BF16)
    return hi, mid, lo


def _split2(a):
    hi = a.astype(BF16)
    lo = (a - hi.astype(F32)).astype(BF16)
    return hi, lo


def _dot(a, b):
    return jnp.dot(a, b, preferred_element_type=F32)


def _dot_nt(a, b):
    return lax.dot_general(a, b, (((1,), (1,)), ((), ())), preferred_element_type=F32)


def _silu(x):
    return x * jax.nn.sigmoid(x)


def _gelu_tanh(x):
    c = math.sqrt(2.0 / math.pi)
    return x * (0.5 * (1.0 + jnp.tanh(c * (x + 0.044715 * (x * x * x)))))


def _softplus(x):
    return jnp.maximum(x, 0.0) + jnp.log1p(jnp.exp(-jnp.abs(x)))


def _rms(x):
    return x * lax.rsqrt(jnp.mean(x * x, axis=-1, keepdims=True) + EPS)


def _mod_kernel(cond_ref, w_ref, b_ref, o_ref):
    c = cond_ref[...]
    s_hi, s_lo = _split2(_silu(c))
    w_hi, w_lo = _split2(w_ref[0])
    o_ref[0] = _dot(s_hi, w_hi) + _dot(s_hi, w_lo) + _dot(s_lo, w_hi) + b_ref[0]


def _modulation(cond, w_mod, b_mod):
    depth, d, n = w_mod.shape
    tn = 1536
    return pl.pallas_call(
        _mod_kernel,
        grid=(depth, n // tn),
        in_specs=[pl.BlockSpec((MOD_ROWS, d), lambda l, j: (0, 0)),
                  pl.BlockSpec((1, d, tn), lambda l, j: (l, 0, j)),
                  pl.BlockSpec((1, 1, tn), lambda l, j: (l, 0, j))],
        out_specs=pl.BlockSpec((1, MOD_ROWS, tn), lambda l, j: (l, 0, j)),
        out_shape=jax.ShapeDtypeStruct((depth, MOD_ROWS, n), F32),
        compiler_params=_cparams(("arbitrary", "arbitrary")),
        name="modulation",
    )(cond, w_mod, b_mod.reshape(depth, 1, n))


def _inproj_kernel(*refs, has_pos, tm):
    if has_pos:
        x_ref, pos_ref, mod_ref, nw_ref, w_ref, wdt_ref, proj_ref, dtr_ref, h_scr = refs
    else:
        x_ref, mod_ref, nw_ref, w_ref, wdt_ref, proj_ref, dtr_ref, h_scr = refs
        pos_ref = None

    @pl.when(pl.program_id(1) == 0)
    def _():
        x = x_ref[...]
        if has_pos:
            x = x + pos_ref[...]
        h = (_rms(x) * nw_ref[...]) * (1.0 + mod_ref[0, 1:2, :]) + mod_ref[0, 0:1, :]
        hb = h.astype(BF16)
        h_scr[...] = hb
        dt = _dot(hb, wdt_ref[...])
        dtt = dt.T
        for g in range(SSD_GROUPS):
            dtr_ref[g] = dtt[g * 16:(g + 1) * 16, :]

    proj_ref[...] = _dot(h_scr[...], w_ref[...]).astype(BF16)


def _inproj(x, pos, mod, norm_w, w_main, w_dt, *, n_tokens):
    t, d = n_tokens, x.shape[1]
    tn = 2304
    nb = mod.shape[0]
    rows_per_mod = t // nb
    tm = _pick_tile(rows_per_mod, 1024)
    has_pos = pos is not None
    in_specs = [pl.BlockSpec((tm, d), lambda i, j: (i, 0))]
    args = [x]
    if has_pos:
        pos_blocks = pos.shape[0] // tm
        in_specs.append(pl.BlockSpec((tm, d), lambda i, j: (i % pos_blocks, 0)))
        args.append(pos)
    in_specs += [pl.BlockSpec((1, N_MOD, d), lambda i, j: ((i * tm) // rows_per_mod, 0, 0)),
                 pl.BlockSpec((1, d), lambda i, j: (0, 0)),
                 pl.BlockSpec((d, tn), lambda i, j: (0, j)),
                 pl.BlockSpec((d, DT_PAD), lambda i, j: (0, 0))]
    args += [mod, norm_w, w_main, w_dt]
    return pl.pallas_call(
        functools.partial(_inproj_kernel, has_pos=has_pos, tm=tm),
        grid=(t // tm, MAIN_COLS // tn),
        in_specs=in_specs,
        out_specs=[pl.BlockSpec((tm, tn), lambda i, j: (i, j)),
                   pl.BlockSpec((SSD_GROUPS, 16, tm), lambda i, j: (0, 0, i))],
        out_shape=[jax.ShapeDtypeStruct((t, MAIN_COLS), BF16),
                   jax.ShapeDtypeStruct((SSD_GROUPS, 16, t), F32)],
        scratch_shapes=[pltpu.VMEM((tm, d), BF16)],
        compiler_params=_cparams(("arbitrary", "arbitrary")),
        name="inproj",
    )(*args)


def _tri_dot_right(a, tri):
    hi, mid, lo = _split3(a)
    return _dot(hi, tri) + _dot(mid, tri) + _dot(lo, tri)


COL_ACS = (0, 16, 32)
COL_E = (48, 64)
COL_W = (80, 96)
N_PIECES = 7
ROW_Q = 0
ROW_LD = 16


def _ssd_kernel(*refs, seq_len, has_h0, emit_state, n_prev):
    it = iter(refs)
    x_ref, b_ref, c_ref, z_ref, dtr_ref = (next(it) for _ in range(5))
    cwx_ref, cwb_ref, cwc_ref, cbx_ref, cbb_ref, cbc_ref = (next(it) for _ in range(6))
    pcol_ref, dsk_ref, nw_ref, fanf_ref, fanb_ref, bcast_ref = (next(it) for _ in range(6))
    h0_ref = next(it) if has_h0 else None
    prev_ref = next(it) if n_prev else None
    y_ref = next(it)
    st_ref = next(it) if emit_state else None
    (pad_scr, xs_scr, cbf_scr, bbf_scr, btf_scr, yb_scr, yf_scr, dall_scr, pc_scr, q_scr, ld_scr,
     rt_scr, col_scr, rq_scr, hf_scr, hb_scr, xbd_scr, yd_scr) = (next(it) for _ in range(18))

    L = seq_len
    nc = L // CHUNK
    W = GROUP_CH + 2 * SSD_STATE
    n_xblk = GROUP_CH // 128

    zeros8 = jnp.zeros((8, W), F32)
    pad_scr[0:8, :] = zeros8
    pad_scr[L + 8:L + 16, :] = zeros8

    def stage(c, carry):
        r0 = pl.multiple_of(c * CHUNK, CHUNK)
        pad_scr[pl.ds(r0 + 8, CHUNK), 0:GROUP_CH] = x_ref[0, pl.ds(r0, CHUNK), :].astype(F32)
        pad_scr[pl.ds(r0 + 8, CHUNK), GROUP_CH:GROUP_CH + SSD_STATE] = b_ref[0, pl.ds(r0, CHUNK), :].astype(F32)
        pad_scr[pl.ds(r0 + 8, CHUNK), GROUP_CH + SSD_STATE:W] = c_ref[0, pl.ds(r0, CHUNK), :].astype(F32)
        return carry

    lax.fori_loop(0, nc, stage, 0)

    cw = [cwx_ref[...][:, k * 128:(k + 1) * 128] for k in range(GROUP_CH // 128)] + [cwb_ref[...], cwc_ref[...]]
    cb = [cbx_ref[...][:, k * 128:(k + 1) * 128] for k in range(GROUP_CH // 128)] + [cbb_ref[...], cbc_ref[...]]
    win_rows = CHUNK + 16

    def conv(c, carry):
        r0 = pl.multiple_of(c * CHUNK, CHUNK)
        for k in range(W // 128):
            win = pad_scr[pl.ds(r0, win_rows), k * 128:(k + 1) * 128]
            acc = win[8:8 + CHUNK, :] * cw[k][2:3, :]
            for tap in (0, 1, 3, 4):
                d = tap - 2
                rolled = pltpu.roll(win, (win_rows - d) % win_rows, 0)
                acc = acc + rolled[8:8 + CHUNK, :] * cw[k][tap:tap + 1, :]
            acc = _silu(acc + cb[k])
            if k < n_xblk:
                xs_scr[pl.ds(r0, CHUNK), k * 128:(k + 1) * 128] = acc
            elif k == n_xblk:
                bbf_scr[pl.ds(r0, CHUNK), :] = acc.astype(BF16)
                btf_scr[c] = acc.T.astype(BF16)
            else:
                cbf_scr[pl.ds(r0, CHUNK), :] = acc.astype(BF16)
        return carry

    lax.fori_loop(0, nc, conv, 0)

    ri = lax.broadcasted_iota(jnp.int32, (CHUNK, CHUNK), 0)
    ci = lax.broadcasted_iota(jnp.int32, (CHUNK, CHUNK), 1)
    below = ci < ri
    above = ci > ri
    tri_ge = jnp.where(ci >= ri, 1.0, 0.0).astype(BF16)
    tri_le = jnp.where(ci <= ri, 1.0, 0.0).astype(BF16)
    left_half = ci < SSD_HEADDIM

    bias_col = pcol_ref[0, :, 0:1]
    aneg_col = -jnp.exp(pcol_ref[0, :, 1:2])
    for c in range(nc):
        dall_scr[c * 16:(c + 1) * 16, :] = _softplus(dtr_ref[0, :, c * CHUNK:(c + 1) * CHUNK] + bias_col)
    d_all = dall_scr[...]
    nr = nc * 16
    a_all = d_all * jnp.concatenate([aneg_col] * nc, axis=0)
    fwd_rows = (lax.broadcasted_iota(jnp.int32, (nr, 1), 0) & SSD_HPG) == 0
    acs = jnp.where(fwd_rows, _tri_dot_right(a_all, tri_ge), _tri_dot_right(a_all, tri_le))
    total = jnp.where(fwd_rows, acs[:, CHUNK - 1:CHUNK], acs[:, 0:1])
    log2e = 1.0 / math.log(2.0)
    for n, piece in enumerate(_split3(acs * log2e)):
        pc_scr[n] = piece.astype(F32)
    for n, piece in enumerate(_split2(jnp.exp(acs)) + _split2(d_all * jnp.exp(total - acs))):
        pc_scr[3 + n] = piece.astype(F32)
    q_scr[...] = (acs - jnp.log(d_all)) * log2e
    ld_scr[...] = jnp.log(d_all + pltpu.roll(d_all, nr - SSD_HPG, 0)) * log2e

    rt_scr[...] = jnp.zeros(rt_scr.shape, F32)
    for c in range(nc):
        rows = slice(c * 16, (c + 1) * 16)
        for n in range(N_PIECES):
            rt_scr[16 * n:16 * (n + 1), :] = pc_scr[n, rows, :]
        col_scr[c] = rt_scr[...].T.astype(BF16)
        rq_scr[c, ROW_Q:ROW_Q + 16, :] = q_scr[rows, :]
        rq_scr[c, ROW_LD:ROW_LD + SSD_HPG, :] = ld_scr[c * 16:c * 16 + SSD_HPG, :]

    def fans(col, fan_ref):
        both = _dot(col, fan_ref[...])
        return both[:, 0:GROUP_CH], both[:, GROUP_CH:2 * GROUP_CH]

    if has_h0:
        hf_scr[...] = h0_ref[0, 0, 0, 0].T
        hb_scr[...] = h0_ref[0, 0, 1, 0].T
    else:
        hf_scr[...] = jnp.zeros((SSD_STATE, GROUP_CH), F32)
        hb_scr[...] = jnp.zeros((SSD_STATE, GROUP_CH), F32)
    xbd_scr[...] = jnp.zeros(xbd_scr.shape, BF16)

    n_half = GROUP_CH // 256

    def sweeps(i, carry):
        passes = ((nc - 1 - i, hb_scr, fanb_ref, yb_scr, 0), (i, hf_scr, fanf_ref, yf_scr, CHUNK - 1))
        staged = []
        for c, h_scr, fan_ref, y_scr, total_row in passes:
            r0 = pl.multiple_of(c * CHUNK, CHUNK)
            e_x, w_x = fans(col_scr[c], fan_ref)
            staged.append((r0, cbf_scr[pl.ds(r0, CHUNK), :], btf_scr[c], e_x, w_x))
        for hf in range(n_half):
            sl = slice(hf * 256, (hf + 1) * 256)
            for (c, h_scr, fan_ref, y_scr, total_row), (r0, cm, bt, e_x, w_x) in zip(passes, staged):
                e2 = e_x[:, sl]
                h = h_scr[:, sl]
                y_scr[pl.ds(r0, CHUNK), sl] = _dot(cm, h.astype(BF16)) * e2
                xw = (xs_scr[pl.ds(r0, CHUNK), sl] * w_x[:, sl]).astype(BF16)
                h_scr[:, sl] = h * e2[total_row:total_row + 1, :] + _dot(bt, xw)
        return carry

    lax.fori_loop(0, nc, sweeps, 0)

    def finish(j, carry):
        work = []
        for u in range(2):
            c = 2 * j + u
            r0 = pl.multiple_of(c * CHUNK, CHUNK)
            cb_mat = _dot_nt(cbf_scr[pl.ds(r0, CHUNK), :], bbf_scr[pl.ds(r0, CHUNK), :])
            acs_b = _dot(col_scr[c], bcast_ref[...])
            work.append((u, r0, acs_b, rq_scr[c], cb_mat))

        for u, r0, col, rq, cb_mat in work:
            for p in range(n_xblk):
                xb = xs_scr[pl.ds(r0, CHUNK), p * 128:(p + 1) * 128].astype(BF16)
                xbd_scr[u, p, 0:CHUNK, 0:SSD_HEADDIM] = xb[:, 0:SSD_HEADDIM]
                xbd_scr[u, p, CHUNK:2 * CHUNK, SSD_HEADDIM:2 * SSD_HEADDIM] = xb[:, SSD_HEADDIM:2 * SSD_HEADDIM]
        for p in range(n_xblk):
            for u, r0, col, rq, cb_mat in work:
                wpair = []
                for hh in (2 * p, 2 * p + 1):
                    pf = col[:, hh * CHUNK:(hh + 1) * CHUNK]
                    pb = col[:, (SSD_HPG + hh) * CHUNK:(SSD_HPG + hh + 1) * CHUNK]
                    qf = rq[ROW_Q + hh:ROW_Q + hh + 1, :]
                    qb = rq[ROW_Q + SSD_HPG + hh:ROW_Q + SSD_HPG + hh + 1, :]
                    ld = rq[ROW_LD + hh:ROW_LD + hh + 1, :]
                    arg = jnp.where(below, pf - qf, jnp.where(above, pb - qb, ld))
                    wpair.append((cb_mat * jnp.exp2(arg)).astype(BF16))
                yd_scr[u, :, p * 128:(p + 1) * 128] = _dot(jnp.concatenate(wpair, axis=1), xbd_scr[u, p])

        for u, r0, col, rq, cb_mat in work:
            ssq = jnp.zeros((CHUNK, 1), F32)
            for hf in range(n_half):
                sl = slice(hf * 256, (hf + 1) * 256)
                y = (yb_scr[pl.ds(r0, CHUNK), sl] + yf_scr[pl.ds(r0, CHUNK), sl] + yd_scr[u, :, sl]
                     + xs_scr[pl.ds(r0, CHUNK), sl] * dsk_ref[:, sl])
                yg = y * _silu(z_ref[0, pl.ds(r0, CHUNK), sl].astype(F32))
                yd_scr[u, :, sl] = yg
                ssq = ssq + jnp.sum(yg * yg, axis=1, keepdims=True)
            scale = lax.rsqrt(ssq * (1.0 / GROUP_CH) + EPS)
            for hf in range(n_half):
                sl = slice(hf * 256, (hf + 1) * 256)
                y_ref[0, pl.ds(r0, CHUNK), sl] = (yd_scr[u, :, sl] * scale * nw_ref[:, sl]).astype(BF16)
        return carry

    lax.fori_loop(0, nc // 2, finish, 0)

    if emit_state:
        for l in range(n_prev):
            st_ref[0, l] = prev_ref[0, l]
        st_ref[0, n_prev, 0, 0] = hf_scr[...].T
        st_ref[0, n_prev, 1, 0] = hb_scr[...].T


def _fan_matrix(first_head):
    m = np.zeros((CHUNK, 2 * GROUP_CH), np.float32)
    for part, bases in enumerate((COL_E, COL_W)):
        for base in bases:
            for j in range(SSD_HPG):
                m[base + first_head + j,
                  part * GROUP_CH + j * SSD_HEADDIM:part * GROUP_CH + (j + 1) * SSD_HEADDIM] = 1.0
    return jnp.asarray(m, BF16)


def _bcast_matrix():
    m = np.zeros((CHUNK, 2 * SSD_HPG * CHUNK), np.float32)
    for base in COL_ACS:
        for j in range(2 * SSD_HPG):
            m[base + j, j * CHUNK:(j + 1) * CHUNK] = 1.0
    return jnp.asarray(m, BF16)


def _ssd(proj, dtr, lw, h0, prev_state, *, batch, seq_len, layer, emit_state):
    L = seq_len
    proj3 = proj.reshape(batch, L, MAIN_COLS)
    has_h0 = h0 is not None
    gc = GROUP_CH
    W = gc + 2 * SSD_STATE

    in_specs = [
        pl.BlockSpec((1, L, gc), lambda b, g: (b, 0, COL_X // gc + g)),
        pl.BlockSpec((1, L, SSD_STATE), lambda b, g: (b, 0, COL_B // SSD_STATE + g)),
        pl.BlockSpec((1, L, SSD_STATE), lambda b, g: (b, 0, COL_C // SSD_STATE + g)),
        pl.BlockSpec((1, L, gc), lambda b, g: (b, 0, COL_Z // gc + g)),
        pl.BlockSpec((1, 16, L), lambda b, g: (g, 0, b)),
        pl.BlockSpec((SSD_CONV, gc), lambda b, g: (0, g)),
        pl.BlockSpec((SSD_CONV, SSD_STATE), lambda b, g: (0, SSD_INNER // SSD_STATE + g)),
        pl.BlockSpec((SSD_CONV, SSD_STATE), lambda b, g: (0, SSD_INNER // SSD_STATE + SSD_GROUPS + g)),
        pl.BlockSpec((1, gc), lambda b, g: (0, g)),
        pl.BlockSpec((1, SSD_STATE), lambda b, g: (0, SSD_INNER // SSD_STATE + g)),
        pl.BlockSpec((1, SSD_STATE), lambda b, g: (0, SSD_INNER // SSD_STATE + SSD_GROUPS + g)),
        pl.BlockSpec((1, 16, 2), lambda b, g: (g, 0, 0)),
        pl.BlockSpec((1, gc), lambda b, g: (0, g)),
        pl.BlockSpec((1, gc), lambda b, g: (0, g)),
        pl.BlockSpec((CHUNK, 2 * gc), lambda b, g: (0, 0)),
        pl.BlockSpec((CHUNK, 2 * gc), lambda b, g: (0, 0)),
        pl.BlockSpec((CHUNK, 2 * SSD_HPG * CHUNK), lambda b, g: (0, 0)),
    ]
    args = [proj3, proj3, proj3, proj3, dtr,
            lw["conv_w"], lw["conv_w"], lw["conv_w"], lw["conv_b"], lw["conv_b"], lw["conv_b"],
            lw["ssd_pcol"], lw["d_skip_x"], lw["ssd_norm_w"], _fan_matrix(0), _fan_matrix(SSD_HPG),
            _bcast_matrix()]
    if has_h0:
        in_specs.append(pl.BlockSpec((1, 1, 2, 1, gc, SSD_STATE), lambda b, g: (b, layer, 0, g, 0, 0)))
        args.append(h0)
    out_specs = [pl.BlockSpec((1, L, gc), lambda b, g: (b, 0, g))]
    out_shape = [jax.ShapeDtypeStruct((batch, L, SSD_INNER), BF16)]
    n_prev = 0 if prev_state is None else prev_state.shape[1]
    if n_prev:
        in_specs.append(pl.BlockSpec((1, n_prev, 2, 1, gc, SSD_STATE), lambda b, g: (b, 0, 0, g, 0, 0)))
        args.append(prev_state)
    if emit_state:
        out_specs.append(pl.BlockSpec((1, n_prev + 1, 2, 1, gc, SSD_STATE), lambda b, g: (b, 0, 0, g, 0, 0)))
        out_shape.append(jax.ShapeDtypeStruct((batch, n_prev + 1, 2, SSD_GROUPS, gc, SSD_STATE), F32))
    nc = L // CHUNK
    scratch = [
        pltpu.VMEM((L + 16, W), F32),
        pltpu.VMEM((L, gc), F32),
        pltpu.VMEM((L, SSD_STATE), BF16),
        pltpu.VMEM((L, SSD_STATE), BF16),
        pltpu.VMEM((nc, SSD_STATE, CHUNK), BF16),
        pltpu.VMEM((L, gc), F32),
        pltpu.VMEM((L, gc), F32),
        pltpu.VMEM((nc * 16, CHUNK), F32),
        pltpu.VMEM((N_PIECES, nc * 16, CHUNK), F32),
        pltpu.VMEM((nc * 16, CHUNK), F32),
        pltpu.VMEM((nc * 16, CHUNK), F32),
        pltpu.VMEM((CHUNK, CHUNK), F32),
        pltpu.VMEM((nc, CHUNK, CHUNK), BF16),
        pltpu.VMEM((nc, 24, CHUNK), F32),
        pltpu.VMEM((SSD_STATE, gc), F32),
        pltpu.VMEM((SSD_STATE, gc), F32),
        pltpu.VMEM((2, gc // 128, 2 * CHUNK, 128), BF16),
        pltpu.VMEM((2, CHUNK, gc), F32),
    ]
    outs = pl.pallas_call(
        functools.partial(_ssd_kernel, seq_len=L, has_h0=has_h0, emit_state=emit_state, n_prev=n_prev),
        grid=(batch, SSD_GROUPS),
        in_specs=in_specs,
        out_specs=out_specs,
        out_shape=out_shape,
        scratch_shapes=scratch,
        compiler_params=_cparams(("arbitrary", "arbitrary")),
        name="ssd",
    )(*args)
    y = outs[0].reshape(batch * L, SSD_INNER)
    return y, (outs[1] if emit_state else None)


def _route(sel, scores):
    neg = -jnp.inf

    def first_max(vals):
        m = vals[0]
        for v in vals[1:]:
            m = jnp.maximum(m, v)
        taken = jnp.zeros_like(m)
        flags = []
        for v in vals:
            f = jnp.where(v == m, 1.0, 0.0) * (1.0 - taken)
            flags.append(f)
            taken = taken + f
        return m, flags

    group_scores = []
    for j in range(N_EXPERT_GROUPS):
        a = sel[j * EXPERTS_PER_GROUP:(j + 1) * EXPERTS_PER_GROUP]
        m1, f1 = first_max(a)
        m2, _ = first_max([jnp.where(f > 0.5, neg, v) for f, v in zip(f1, a)])
        group_scores.append(m1 + m2)
    _, gflag = first_max(group_scores)
    masked = [jnp.where(gflag[e // EXPERTS_PER_GROUP] > 0.5, sel[e], neg) for e in range(N_EXPERTS)]
    _, f1 = first_max(masked)
    _, f2 = first_max([jnp.where(f > 0.5, neg, v) for f, v in zip(f1, masked)])
    w1 = sum(f * s for f, s in zip(f1, scores))
    w2 = sum(f * s for f, s in zip(f2, scores))
    tot = w1 + w2
    gates = [(f1[e] * w1 + f2[e] * w2) / tot for e in range(N_EXPERTS)]
    taken = jnp.zeros_like(tot)
    e_lo = e_hi = g_lo = g_hi = jnp.zeros_like(tot)
    for e in range(N_EXPERTS):
        sel_e = f1[e] + f2[e]
        low = sel_e * (1.0 - taken)
        high = sel_e - low
        taken = taken + sel_e
        e_lo = e_lo + float(e) * low
        e_hi = e_hi + float(e) * high
        g_lo = g_lo + low * gates[e]
        g_hi = g_hi + high * gates[e]
    group = sum(float(j) * gflag[j] for j in range(1, N_EXPERT_GROUPS))
    a = e_lo - EXPERTS_PER_GROUP * group
    b = e_hi - EXPERTS_PER_GROUP * group
    pair = a * (2 * EXPERTS_PER_GROUP - 1 - a) * 0.5 + (b - a - 1.0)
    bucket = group * PAIRS_PER_GROUP + pair
    return (g_lo, g_hi), bucket


def _mixout_kernel(*refs, has_pos, tm, rows_per_mod):
    it = iter(refs)
    x_ref = next(it)
    pos_ref = next(it) if has_pos else None
    ga_ref, gb_ref, u_ref, v_ref, yb_ref, mod_ref = (next(it) for _ in range(6))
    gnw_ref, ws_ref, bsx_ref, wa_ref, wb_ref, wo_ref, n2w_ref, rwt_ref, rb_ref = (next(it) for _ in range(9))
    rows_ref, gid_ref = (next(it) for _ in range(2))
    g_scr = next(it)

    sub = min(tm, MIX_SUB)
    spans = [slice(s * sub, (s + 1) * sub) for s in range(tm // sub)]
    n_chunks = tm // CHUNK
    gd = GMLP_WIDTH // GMLP_GROUPS

    u = [_gelu_tanh(u_ref[r, :].astype(F32)) for r in spans]
    vn = [(_rms(_gelu_tanh(v_ref[r, :].astype(F32))) * gnw_ref[...]).astype(BF16) for r in spans]

    def chunk_rows(c):
        s, local = divmod(c * CHUNK, sub)
        return s, slice(local, local + CHUNK)

    mixed_cols = [[None] * GMLP_GROUPS for _ in range(n_chunks)]
    for g in range(GMLP_GROUPS):
        pieces = []
        for c in range(n_chunks):
            s, rr = chunk_rows(c)
            pieces.append(vn[s][rr, g * gd:(g + 1) * gd])
        res = _dot(ws_ref[g], jnp.concatenate(pieces, axis=1))
        for c in range(n_chunks):
            mixed_cols[c][g] = res[:, c * gd:(c + 1) * gd]
    per_sub = sub // CHUNK
    ya = []
    for s in range(len(spans)):
        mixed = jnp.concatenate([jnp.concatenate(mixed_cols[s * per_sub + c], axis=1) + bsx_ref[...]
                                 for c in range(per_sub)], axis=0)
        ya.append((u[s] * mixed).astype(BF16))

    da = [_dot(y, wa_ref[...]) for y in ya]
    db = [_dot(yb_ref[r, :], wb_ref[...]) for r in spans]
    merged = [(jax.nn.sigmoid(ga_ref[r, :].astype(F32)) * a + jax.nn.sigmoid(gb_ref[r, :].astype(F32)) * b
               ).astype(BF16) for r, a, b in zip(spans, da, db)]
    do = [_dot(m, wo_ref[...]) for m in merged]

    h2s = []
    for r, o in zip(spans, do):
        x = x_ref[r, :]
        if has_pos:
            x = x + pos_ref[r, :]
        xn = x + mod_ref[0, 2:3, :] * o
        rows_ref[r, ROW_X:ROW_X + D_MODEL] = xn
        h2 = (_rms(xn) * n2w_ref[...]) * (1.0 + mod_ref[0, 4:5, :]) + mod_ref[0, 3:4, :]
        rows_ref[r, ROW_H:ROW_H + D_MODEL] = h2
        h2s.append(h2)

    r_hi, r_lo = _split2(rwt_ref[...])
    g_scr[...] = jnp.zeros(g_scr.shape, F32)
    mod_row = (pl.program_id(0) * tm) // rows_per_mod
    g_scr[AUX_MODROW:AUX_MODROW + 1, :] = jnp.full((1, tm), mod_row, jnp.int32).astype(F32)
    for r, h2 in zip(spans, h2s):
        h_hi, h_lo = _split2(h2)
        logits = _dot_nt(r_hi, h_hi) + _dot_nt(r_hi, h_lo) + _dot_nt(r_lo, h_hi)
        scores = jax.nn.sigmoid(logits)
        selm = scores + rb_ref[...]
        pair_gates, bucket = _route([selm[e:e + 1, :] for e in range(N_EXPERTS)],
                                    [scores[e:e + 1, :] for e in range(N_EXPERTS)])
        gid_ref[:, r] = bucket.astype(jnp.int32)
        for k in range(2):
            g_scr[k:k + 1, r] = pair_gates[k]
    rows_ref[:, ROW_AUX:ROW_W] = g_scr[...].T


def _mixout(x, pos, proj, yb, mod, lw, *, n_tokens):
    t, d = n_tokens, x.shape[1]
    nb = mod.shape[0]
    rows_per_mod = t // nb
    tm = _pick_tile(rows_per_mod, MIX_TILE)
    has_pos = pos is not None
    full = lambda shape: pl.BlockSpec(shape, lambda i: (0,) * len(shape), pipeline_mode=pl.Buffered(1))
    in_specs = [pl.BlockSpec((tm, d), lambda i: (i, 0))]
    args = [x]
    if has_pos:
        pos_blocks = pos.shape[0] // tm
        in_specs.append(pl.BlockSpec((tm, d), lambda i: (i % pos_blocks, 0)))
        args.append(pos)
    in_specs += [pl.BlockSpec((tm, d), lambda i: (i, 0)),
                 pl.BlockSpec((tm, d), lambda i: (i, 1)),
                 pl.BlockSpec((tm, d), lambda i: (i, 2)),
                 pl.BlockSpec((tm, d), lambda i: (i, 3)),
                 pl.BlockSpec((tm, SSD_INNER), lambda i: (i, 0)),
                 pl.BlockSpec((1, N_MOD, d), lambda i: ((i * tm) // rows_per_mod, 0, 0)),
                 full((1, GMLP_WIDTH)),
                 full((GMLP_GROUPS, CHUNK, CHUNK)),
                 full((CHUNK, GMLP_WIDTH)),
                 full((GMLP_WIDTH, d)),
                 full((SSD_INNER, d)),
                 full((d, d)),
                 full((1, d)),
                 full((N_EXPERTS, d)),
                 full((N_EXPERTS, 1))]
    args += [proj, proj, proj, proj, yb, mod,
             lw["gmlp_norm_w"], lw["gmlp_ws"], lw["gmlp_bs_x"], lw["w_branch_a"], lw["w_branch_b"], lw["w_out"],
             lw["norm2_w"], lw["router_wt"], lw["router_bias"]]
    return pl.pallas_call(
        functools.partial(_mixout_kernel, has_pos=has_pos, tm=tm, rows_per_mod=rows_per_mod),
        grid=(t // tm,),
        in_specs=in_specs,
        out_specs=[pl.BlockSpec((tm, ROW_W), lambda i: (i, 0)),
                   pl.BlockSpec((1, tm), lambda i: (0, i))],
        out_shape=[jax.ShapeDtypeStruct((t, ROW_W), F32),
                   jax.ShapeDtypeStruct((1, t), jnp.int32)],
        scratch_shapes=[pltpu.VMEM((ROW_W - ROW_AUX, tm), F32)],
        compiler_params=_cparams(("arbitrary",)),
        name="mixout",
    )(*args)


def _moe_plan(gid, n_tokens, tm):
    nb = N_BUCKETS
    t = n_tokens
    i32 = jnp.int32
    onehot = (gid[:, None] == jnp.arange(nb, dtype=i32)[None, :]).astype(i32)
    csum = jnp.cumsum(onehot, axis=0)
    rank = jnp.sum(onehot * (csum - 1), axis=1)
    counts = csum[-1]
    padded = ((counts + tm - 1) // tm) * tm
    ends = jnp.cumsum(padded)
    dest = jnp.sum(onehot * (ends - padded)[None, :], axis=1) + rank
    nt = pl.cdiv(t, tm) + nb
    tile_bucket = jnp.sum((jnp.arange(nt, dtype=i32) * tm)[:, None] >= ends[None, :], axis=1)
    tile_bucket = jnp.minimum(tile_bucket, nb - 1)
    pairs = [(a, b) for a in range(EXPERTS_PER_GROUP) for b in range(a + 1, EXPERTS_PER_GROUP)]
    lo = jnp.asarray([g * EXPERTS_PER_GROUP + a for g in range(N_EXPERT_GROUPS) for a, _ in pairs], i32)
    hi = jnp.asarray([g * EXPERTS_PER_GROUP + b for g in range(N_EXPERT_GROUPS) for _, b in pairs], i32)
    n_used = (ends[-1] // tm).astype(i32)
    tail = n_used + jnp.arange(nb, dtype=i32)
    cand = jnp.concatenate([jnp.where(padded > 0, ends - tm, -1), jnp.where(tail < nt, tail * tm, -1)])
    keep = cand >= 0
    order = jnp.argsort(jnp.logical_not(keep), stable=True)
    fill = jnp.concatenate([jnp.sum(keep).reshape(1), cand[order]]).astype(i32)
    return dest.astype(i32), lo[tile_bucket], hi[tile_bucket], n_used.reshape(1), fill


def _permute_kernel(*refs, tn, scatter, fill_rows):
    if fill_rows:
        fill_ref, idx_ref, src_ref, out_ref, zbuf, sem, zsem = refs
    else:
        idx_ref, src_ref, out_ref, sem = refs
    out_hbm = out_ref
    i = pl.program_id(0)

    if fill_rows:
        @pl.when(i == 0)
        def _():
            zbuf[...] = jnp.zeros(zbuf.shape, F32)
            n_fill = fill_ref[0]

            def start_fill(k, carry):
                start = pl.multiple_of(fill_ref[1 + k], fill_rows)
                pltpu.make_async_copy(zbuf, out_hbm.at[pl.ds(start, fill_rows), :], zsem).start()
                return carry

            def wait_fill(k, carry):
                pltpu.make_async_copy(zbuf, out_hbm.at[pl.ds(0, fill_rows), :], zsem).wait()
                return carry

            lax.fori_loop(0, n_fill, start_fill, 0)
            lax.fori_loop(0, n_fill, wait_fill, 0)

    group = 16

    def body(j, carry):
        r0 = pl.multiple_of(j * group, group)
        for k in range(group):
            p = idx_ref[0, 0, r0 + k]
            t8, s8 = j * (group // 8) + k // 8, k % 8
            if scatter:
                cp = pltpu.make_async_copy(src_ref.at[t8, pl.ds(s8, 1), :], out_ref.at[pl.ds(p, 1), :], sem)
            else:
                cp = pltpu.make_async_copy(src_ref.at[pl.ds(p, 1), :], out_ref.at[t8, pl.ds(s8, 1), :], sem)
            cp.start()
        return carry

    lax.fori_loop(0, tn // group, body, 0)
    block_ref = src_ref if scatter else out_ref
    pltpu.make_async_copy(block_ref, block_ref, sem).wait()


def _permute_rows(src, idx, *, n_rows, n_out, scatter, fill=None, fill_rows=0):
    w = src.shape[1]
    tn = _pick_tile(n_rows, 1024)
    steps = n_rows // tn
    idx3 = idx.reshape(steps, 1, tn)
    block = pl.BlockSpec((tn // 8, 8, w), lambda i, *_: (i, 0, 0))
    hbm = pl.BlockSpec(memory_space=pl.ANY)
    if scatter:
        src = src.reshape(src.shape[0] // 8, 8, w)
    in_specs = [pl.BlockSpec((1, 1, tn), lambda i, *_: (i, 0, 0), memory_space=pltpu.SMEM),
                block if scatter else hbm]
    scratch = [pltpu.SemaphoreType.DMA(())]
    args = [idx3, src]
    n_prefetch = 0
    if fill_rows:
        n_prefetch = 1
        args = [fill] + args
        scratch = [pltpu.VMEM((fill_rows, w), F32), pltpu.SemaphoreType.DMA(()), pltpu.SemaphoreType.DMA(())]
    out = pl.pallas_call(
        functools.partial(_permute_kernel, tn=tn, scatter=scatter, fill_rows=fill_rows),
        grid_spec=pltpu.PrefetchScalarGridSpec(
            num_scalar_prefetch=n_prefetch, grid=(steps,), in_specs=in_specs,
            out_specs=hbm if scatter else block, scratch_shapes=scratch),
        out_shape=jax.ShapeDtypeStruct((n_out, w) if scatter else (n_out // 8, 8, w), F32),
        compiler_params=_cparams(("arbitrary",)),
        name="permute_scatter" if scatter else "permute_gather",
    )(*args)
    return out if scatter else out.reshape(n_out, w)


def _moe_kernel(elo_ref, ehi_ref, nused_ref, xs_ref, g2_ref, wg0_ref, wu0_ref, wd0_ref, wg1_ref, wu1_ref, wd1_ref,
                fnw_ref, o_ref, *, tm, final, n_mod):
    used = pl.program_id(0) < nused_ref[0]

    @pl.when(jnp.logical_not(used))
    def _():
        o_ref[...] = jnp.zeros(o_ref.shape, F32)

    @pl.when(used)
    def _():
        rows = xs_ref[...]
        h = rows[:, ROW_H:ROW_H + D_MODEL].astype(BF16)
        y = None
        for k, (wg_ref, wu_ref, wd_ref) in enumerate(((wg0_ref, wu0_ref, wd0_ref), (wg1_ref, wu1_ref, wd1_ref))):
            act = _silu(_dot(h, wg_ref[0])) * _dot(h, wu_ref[0])
            gate = rows[:, ROW_AUX + k:ROW_AUX + k + 1]
            part = _dot((act * gate).astype(BF16), wd_ref[0])
            y = part if y is None else y + part
        if n_mod == 1:
            g2 = g2_ref[0:1, :]
        else:
            n = g2_ref.shape[0]
            mod_row = rows[:, ROW_AUX + AUX_MODROW:ROW_AUX + AUX_MODROW + 1]
            ids = lax.broadcasted_iota(jnp.int32, (tm, 2 * n), 1)
            ids = jnp.where(ids >= n, ids - n, ids).astype(F32)
            onehot = jnp.where(mod_row == ids, 1.0, 0.0).astype(BF16)
            g2 = _dot(onehot, jnp.concatenate(_split2(g2_ref[...]), axis=0))
        xn = rows[:, ROW_X:ROW_X + D_MODEL] + g2 * y
        if final:
            xn = _rms(xn) * fnw_ref[...]
        o_ref[...] = xn


def _moe(rows, gid, g2, lw, final_norm_w, *, n_tokens, final):
    t, d, tm = n_tokens, D_MODEL, MOE_TILE
    dest, tile_lo, tile_hi, n_used, fill = _moe_plan(gid.reshape(t), t, tm)
    nt = tile_lo.shape[0]
    n_sorted = nt * tm
    n_mod = g2.shape[0]
    if n_mod > 1:
        g2 = jnp.pad(g2, ((0, (-n_mod) % 8), (0, 0)))
    xs = _permute_rows(rows, dest, n_rows=t, n_out=n_sorted, scatter=True, fill=fill, fill_rows=tm)

    def tile(i, nu):
        return jnp.minimum(i, nu[0] - 1)

    def expert_specs(which):
        def idx(i, elo, ehi, nu):
            return ((elo, ehi)[which][tile(i, nu)], 0, 0)
        return [pl.BlockSpec((1, d, D_EXPERT), idx), pl.BlockSpec((1, d, D_EXPERT), idx),
                pl.BlockSpec((1, D_EXPERT, d), idx)]

    grid_spec = pltpu.PrefetchScalarGridSpec(
        num_scalar_prefetch=3,
        grid=(nt,),
        in_specs=[pl.BlockSpec((tm, ROW_W), lambda i, elo, ehi, nu: (tile(i, nu), 0)),
                  pl.BlockSpec(g2.shape, lambda i, elo, ehi, nu: (0, 0))]
                 + expert_specs(0) + expert_specs(1)
                 + [pl.BlockSpec((1, d), lambda i, elo, ehi, nu: (0, 0))],
        out_specs=pl.BlockSpec((tm, d), lambda i, elo, ehi, nu: (i, 0)))
    we = (lw["w_gate_e"], lw["w_up_e"], lw["w_down_e"])
    ys = pl.pallas_call(
        functools.partial(_moe_kernel, tm=tm, final=final, n_mod=n_mod),
        grid_spec=grid_spec,
        out_shape=jax.ShapeDtypeStruct((n_sorted, d), F32),
        compiler_params=_cparams(("arbitrary",)),
        name="moe",
    )(tile_lo, tile_hi, n_used, xs, g2, *we, *we, final_norm_w)
    return _permute_rows(ys, dest, n_rows=t, n_out=t, scatter=False)


def _grid_pos_embed(n_tokens, dim):
    rows = n_tokens // GRID_W
    quarter = dim // 4
    omega = 1.0 / (10000.0 ** (jnp.arange(quarter, dtype=F32) / quarter))
    r = jnp.arange(rows, dtype=F32)[:, None] * omega
    col = jnp.arange(GRID_W, dtype=F32)[:, None] * omega
    r_emb = jnp.concatenate([jnp.sin(r), jnp.cos(r)], axis=-1)
    c_emb = jnp.concatenate([jnp.sin(col), jnp.cos(col)], axis=-1)
    emb = jnp.concatenate([
        jnp.broadcast_to(r_emb[:, None, :], (rows, GRID_W, dim // 2)),
        jnp.broadcast_to(c_emb[None, :, :], (rows, GRID_W, dim // 2))], axis=-1)
    return emb.reshape(rows * GRID_W, dim)


def _dt_perm():
    return np.array([dr * SSD_HEADS + g * SSD_HPG + j
                     for g in range(SSD_GROUPS) for dr in range(2) for j in range(SSD_HPG)], np.int32)


def _layer_weights(l, w_in, norm1_w, gmlp_norm_w, gmlp_ws, gmlp_bs, conv_w, conv_b, dt_bias, a_log, d_skip,
                   ssd_norm_w, w_branch_a, w_branch_b, w_out, norm2_w, router_w, router_bias,
                   w_gate_e, w_up_e, w_down_e):
    perm = _dt_perm()
    w_dt = jnp.pad(w_in[l][:, MAIN_COLS:][:, perm], ((0, 0), (0, DT_PAD - 2 * SSD_HEADS)))
    prow = jnp.stack([dt_bias[l].reshape(-1)[perm], a_log[l].reshape(-1)[perm]], axis=0)
    prow = prow.reshape(2, SSD_GROUPS, 16).transpose(1, 0, 2)
    return {
        "norm1_w": norm1_w[l][None],
        "w_main": w_in[l][:, :MAIN_COLS].astype(BF16),
        "w_dt": w_dt.astype(BF16),
        "gmlp_norm_w": gmlp_norm_w[l][None],
        "gmlp_ws": gmlp_ws[l].astype(BF16),
        "gmlp_bs_x": jnp.repeat(gmlp_bs[l].T, GMLP_WIDTH // GMLP_GROUPS, axis=1),
        "conv_w": conv_w[l],
        "conv_b": conv_b[l][None],
        "ssd_pcol": prow.transpose(0, 2, 1),
        "d_skip_x": jnp.repeat(d_skip[l], SSD_HEADDIM)[None],
        "ssd_norm_w": ssd_norm_w[l][None],
        "w_branch_a": w_branch_a[l].astype(BF16),
        "w_branch_b": w_branch_b[l].astype(BF16),
        "w_out": w_out[l].astype(BF16),
        "norm2_w": norm2_w[l][None],
        "router_wt": router_w.T,
        "router_bias": router_bias[:, None],
        "w_gate_e": w_gate_e[l].astype(BF16),
        "w_up_e": w_up_e[l].astype(BF16),
        "w_down_e": w_down_e[l].astype(BF16),
    }


def _layer(x, pos, mod, lw, h0, prev_state, final_norm_w, *, batch, seq_len, layer, emit_state, final):
    t = batch * seq_len
    proj, dtr = _inproj(x, pos, mod, lw["norm1_w"], lw["w_main"], lw["w_dt"], n_tokens=t)
    yb, state = _ssd(proj, dtr, lw, h0, prev_state, batch=batch, seq_len=seq_len, layer=layer,
                     emit_state=emit_state)
    rows, gid = _mixout(x, pos, proj, yb, mod, lw, n_tokens=t)
    out = _moe(rows, gid, mod[:, N_MOD - 1, :], lw, final_norm_w, n_tokens=t, final=final)
    return out, state


def kernel(x_prompt, x_sample, state_ssd, c, c_ctx, w_mod, b_mod, norm1_w, w_in, gmlp_norm_w, gmlp_ws, gmlp_bs, conv_w, conv_b, dt_bias, a_log, d_skip, ssd_norm_w, w_branch_a, w_branch_b, w_out, norm2_w, router_w, router_bias, w_gate_e, w_up_e, w_down_e, final_norm_w):
    bp, lp, d = x_prompt.shape
    bs, ls, _ = x_sample.shape
    depth = w_mod.shape[0]
    assert 1 + bs <= MOD_ROWS and d == D_MODEL
    assert lp % (2 * CHUNK) == 0 and ls % (2 * CHUNK) == 0

    cond = jnp.concatenate([c_ctx[None], c, jnp.zeros((MOD_ROWS - 1 - bs, d), F32)], axis=0)
    mod = _modulation(cond, w_mod, b_mod).reshape(depth, MOD_ROWS, N_MOD, d)

    pos = _grid_pos_embed(ls, d)
    xp = x_prompt.reshape(bp * lp, d)
    xs = x_sample.reshape(bs * ls, d)
    cache = state_ssd.reshape(bs, depth, 2, SSD_GROUPS, GROUP_CH, SSD_STATE)
    fnw = final_norm_w[None]

    states = None
    for l in range(depth):
        lw = _layer_weights(l, w_in, norm1_w, gmlp_norm_w, gmlp_ws, gmlp_bs, conv_w, conv_b, dt_bias, a_log,
                            d_skip, ssd_norm_w, w_branch_a, w_branch_b, w_out, norm2_w, router_w, router_bias,
                            w_gate_e, w_up_e, w_down_e)
        final = l == depth - 1
        xp, states = _layer(xp, None, mod[l, 0:1], lw, None, states, fnw, batch=bp, seq_len=lp, layer=l,
                            emit_state=True, final=final)
        xs, _ = _layer(xs, pos if l == 0 else None, mod[l, 1:1 + bs], lw, cache, None, fnw, batch=bs, seq_len=ls,
                       layer=l, emit_state=False, final=final)

    new_state = states.reshape(bp, depth, 2, SSD_HEADS, SSD_HEADDIM, SSD_STATE)
    return (xp.reshape(bp, lp, d), xs.reshape(bs, ls, d), new_state.astype(x_prompt.dtype))
```

```python
import functools
import math

import numpy as np
import jax
import jax.numpy as jnp
from jax import lax
from jax.experimental import pallas as pl
from jax.experimental.pallas import tpu as pltpu

F32 = jnp.float32
BF16 = jnp.bfloat16

D_MODEL = 1024
CHUNK = 128
GRID_W = 64
GMLP_WIDTH = 1024
GMLP_GROUPS = 8
SSD_INNER = 2048
SSD_HEADDIM = 64
SSD_HEADS = 32
SSD_GROUPS = 4
SSD_HPG = 8
SSD_STATE = 128
SSD_CONV = 5
GROUP_CH = SSD_INNER // SSD_GROUPS
N_EXPERTS = 16
N_EXPERT_GROUPS = 4
EXPERTS_PER_GROUP = 4
D_EXPERT = 512
N_MOD = 6
EPS = 1e-6
MAIN_COLS = 2 * D_MODEL + 2 * GMLP_WIDTH + SSD_INNER + SSD_INNER + 2 * SSD_GROUPS * SSD_STATE
COL_Z = 2 * D_MODEL + 2 * GMLP_WIDTH
COL_X = COL_Z + SSD_INNER
COL_B = COL_X + SSD_INNER
COL_C = COL_B + SSD_GROUPS * SSD_STATE
DT_PAD = 128
ROW_X = 0
ROW_H = D_MODEL
ROW_AUX = 2 * D_MODEL
PAIRS_PER_GROUP = 6
N_BUCKETS = 24
ROW_W = 2 * D_MODEL + 128
AUX_MODROW = 4
MOE_TILE = 256
MIX_SUB = 128
MIX_TILE = 512
MOD_ROWS = 16

V7X_VMEM_LIMIT = 56 * 1024 * 1024


def _cparams(sem):
    return pltpu.CompilerParams(dimension_semantics=sem, vmem_limit_bytes=V7X_VMEM_LIMIT)


def _pick_tile(rows, preferred):
    tm = preferred
    while rows % tm:
        tm //= 2
    assert tm >= CHUNK
    return tm


def _split3(a):
    hi = a.astype(BF16)
    r1 = a - hi.astype(F32)
    mid = r1.astype(BF16)
    lo = (r1 - mid.astype(F32)).astype(BF16)
    return hi, mid, lo


def _split2(a):
    hi = a.astype(BF16)
    lo = (a - hi.astype(F32)).astype(BF16)
    return hi, lo


def _dot(a, b):
    return jnp.dot(a, b, preferred_element_type=F32)


def _dot_nt(a, b):
    return lax.dot_general(a, b, (((1,), (1,)), ((), ())), preferred_element_type=F32)


def _silu(x):
    return x * jax.nn.sigmoid(x)


def _gelu_tanh(x):
    c = math.sqrt(2.0 / math.pi)
    return x * (0.5 * (1.0 + jnp.tanh(c * (x + 0.044715 * (x * x * x)))))


def _softplus(x):
    return jnp.maximum(x, 0.0) + jnp.log1p(jnp.exp(-jnp.abs(x)))


def _rms(x):
    return x * lax.rsqrt(jnp.mean(x * x, axis=-1, keepdims=True) + EPS)


def _mod_kernel(cond_ref, w_ref, b_ref, o_ref):
    c = cond_ref[...]
    s_hi, s_lo = _split2(_silu(c))
    w_hi, w_lo = _split2(w_ref[0])
    o_ref[0] = _dot(s_hi, w_hi) + _dot(s_hi, w_lo) + _dot(s_lo, w_hi) + b_ref[0]


def _modulation(cond, w_mod, b_mod):
    depth, d, n = w_mod.shape
    tn = 1536
    return pl.pallas_call(
        _mod_kernel,
        grid=(depth, n // tn),
        in_specs=[pl.BlockSpec((MOD_ROWS, d), lambda l, j: (0, 0)),
                  pl.BlockSpec((1, d, tn), lambda l, j: (l, 0, j)),
                  pl.BlockSpec((1, 1, tn), lambda l, j: (l, 0, j))],
        out_specs=pl.BlockSpec((1, MOD_ROWS, tn), lambda l, j: (l, 0, j)),
        out_shape=jax.ShapeDtypeStruct((depth, MOD_ROWS, n), F32),
        compiler_params=_cparams(("arbitrary", "arbitrary")),
        name="modulation",
    )(cond, w_mod, b_mod.reshape(depth, 1, n))


def _inproj_kernel(*refs, has_pos, tm):
    if has_pos:
        x_ref, pos_ref, mod_ref, nw_ref, w_ref, wdt_ref, proj_ref, dtr_ref, h_scr = refs
    else:
        x_ref, mod_ref, nw_ref, w_ref, wdt_ref, proj_ref, dtr_ref, h_scr = refs
        pos_ref = None

    @pl.when(pl.program_id(1) == 0)
    def _():
        x = x_ref[...]
        if has_pos:
            x = x + pos_ref[...]
        h = (_rms(x) * nw_ref[...]) * (1.0 + mod_ref[0, 1:2, :]) + mod_ref[0, 0:1, :]
        hb = h.astype(BF16)
        h_scr[...] = hb
        dt = _dot(hb, wdt_ref[...])
        dtt = dt.T
        for g in range(SSD_GROUPS):
            dtr_ref[g] = dtt[g * 16:(g + 1) * 16, :]

    proj_ref[...] = _dot(h_scr[...], w_ref[...]).astype(BF16)


def _inproj(x, pos, mod, norm_w, w_main, w_dt, *, n_tokens):
    t, d = n_tokens, x.shape[1]
    tn = 3072
    nb = mod.shape[0]
    rows_per_mod = t // nb
    tm = _pick_tile(rows_per_mod, 1024)
    has_pos = pos is not None
    in_specs = [pl.BlockSpec((tm, d), lambda i, j: (i, 0))]
    args = [x]
    if has_pos:
        pos_blocks = pos.shape[0] // tm
        in_specs.append(pl.BlockSpec((tm, d), lambda i, j: (i % pos_blocks, 0)))
        args.append(pos)
    in_specs += [pl.BlockSpec((1, N_MOD, d), lambda i, j: ((i * tm) // rows_per_mod, 0, 0)),
                 pl.BlockSpec((1, d), lambda i, j: (0, 0)),
                 pl.BlockSpec((d, tn), lambda i, j: (0, j)),
                 pl.BlockSpec((d, DT_PAD), lambda i, j: (0, 0))]
    args += [mod, norm_w, w_main, w_dt]
    return pl.pallas_call(
        functools.partial(_inproj_kernel, has_pos=has_pos, tm=tm),
        grid=(t // tm, MAIN_COLS // tn),
        in_specs=in_specs,
        out_specs=[pl.BlockSpec((tm, tn), lambda i, j: (i, j)),
                   pl.BlockSpec((SSD_GROUPS, 16, tm), lambda i, j: (0, 0, i))],
        out_shape=[jax.ShapeDtypeStruct((t, MAIN_COLS), BF16),
                   jax.ShapeDtypeStruct((SSD_GROUPS, 16, t), F32)],
        scratch_shapes=[pltpu.VMEM((tm, d), BF16)],
        compiler_params=_cparams(("arbitrary", "arbitrary")),
        name="inproj",
    )(*args)


def _tri_dot_right(a, tri):
    hi, mid, lo = _split3(a)
    return _dot(hi, tri) + _dot(mid, tri) + _dot(lo, tri)


COL_ACS = (0, 16, 32)
COL_E = (48, 64)
COL_W = (80, 96)
N_PIECES = 7
ROW_Q = 0
ROW_LD = 16


def _ssd_kernel(*refs, seq_len, has_h0, emit_state, n_prev):
    it = iter(refs)
    x_ref, b_ref, c_ref, z_ref, dtr_ref = (next(it) for _ in range(5))
    cwx_ref, cwb_ref, cwc_ref, cbx_ref, cbb_ref, cbc_ref = (next(it) for _ in range(6))
    pcol_ref, dsk_ref, nw_ref, fanf_ref, fanb_ref, bcast_ref = (next(it) for _ in range(6))
    h0_ref = next(it) if has_h0 else None
    prev_ref = next(it) if n_prev else None
    y_ref = next(it)
    st_ref = next(it) if emit_state else None
    (pad_scr, xs_scr, cbf_scr, bbf_scr, btf_scr, yb_scr, yf_scr, dall_scr, pc_scr, q_scr, ld_scr,
     rt_scr, col_scr, rq_scr, hf_scr, hb_scr, xbd_scr, yd_scr) = (next(it) for _ in range(18))

    L = seq_len
    nc = L // CHUNK
    W = GROUP_CH + 2 * SSD_STATE
    n_xblk = GROUP_CH // 128

    zeros8 = jnp.zeros((8, W), F32)
    pad_scr[0:8, :] = zeros8
    pad_scr[L + 8:L + 16, :] = zeros8

    def stage(c, carry):
        r0 = pl.multiple_of(c * CHUNK, CHUNK)
        pad_scr[pl.ds(r0 + 8, CHUNK), 0:GROUP_CH] = x_ref[0, pl.ds(r0, CHUNK), :].astype(F32)
        pad_scr[pl.ds(r0 + 8, CHUNK), GROUP_CH:GROUP_CH + SSD_STATE] = b_ref[0, pl.ds(r0, CHUNK), :].astype(F32)
        pad_scr[pl.ds(r0 + 8, CHUNK), GROUP_CH + SSD_STATE:W] = c_ref[0, pl.ds(r0, CHUNK), :].astype(F32)
        return carry

    lax.fori_loop(0, nc, stage, 0)

    cw = [cwx_ref[...][:, k * 128:(k + 1) * 128] for k in range(GROUP_CH // 128)] + [cwb_ref[...], cwc_ref[...]]
    cb = [cbx_ref[...][:, k * 128:(k + 1) * 128] for k in range(GROUP_CH // 128)] + [cbb_ref[...], cbc_ref[...]]
    win_rows = CHUNK + 16

    def conv(c, carry):
        r0 = pl.multiple_of(c * CHUNK, CHUNK)
        for k in range(W // 128):
            win = pad_scr[pl.ds(r0, win_rows), k * 128:(k + 1) * 128]
            acc = win[8:8 + CHUNK, :] * cw[k][2:3, :]
            for tap in (0, 1, 3, 4):
                d = tap - 2
                rolled = pltpu.roll(win, (win_rows - d) % win_rows, 0)
                acc = acc + rolled[8:8 + CHUNK, :] * cw[k][tap:tap + 1, :]
            acc = _silu(acc + cb[k])
            if k < n_xblk:
                xs_scr[pl.ds(r0, CHUNK), k * 128:(k + 1) * 128] = acc
            elif k == n_xblk:
                bbf_scr[pl.ds(r0, CHUNK), :] = acc.astype(BF16)
                btf_scr[c] = acc.T.astype(BF16)
            else:
                cbf_scr[pl.ds(r0, CHUNK), :] = acc.astype(BF16)
        return carry

    lax.fori_loop(0, nc, conv, 0)

    ri = lax.broadcasted_iota(jnp.int32, (CHUNK, CHUNK), 0)
    ci = lax.broadcasted_iota(jnp.int32, (CHUNK, CHUNK), 1)
    below = ci < ri
    above = ci > ri
    tri_ge = jnp.where(ci >= ri, 1.0, 0.0).astype(BF16)
    tri_le = jnp.where(ci <= ri, 1.0, 0.0).astype(BF16)
    left_half = ci < SSD_HEADDIM

    bias_col = pcol_ref[0, :, 0:1]
    aneg_col = -jnp.exp(pcol_ref[0, :, 1:2])
    for c in range(nc):
        dall_scr[c * 16:(c + 1) * 16, :] = _softplus(dtr_ref[0, :, c * CHUNK:(c + 1) * CHUNK] + bias_col)
    d_all = dall_scr[...]
    nr = nc * 16
    a_all = d_all * jnp.concatenate([aneg_col] * nc, axis=0)
    fwd_rows = (lax.broadcasted_iota(jnp.int32, (nr, 1), 0) & SSD_HPG) == 0
    acs = jnp.where(fwd_rows, _tri_dot_right(a_all, tri_ge), _tri_dot_right(a_all, tri_le))
    total = jnp.where(fwd_rows, acs[:, CHUNK - 1:CHUNK], acs[:, 0:1])
    log2e = 1.0 / math.log(2.0)
    for n, piece in enumerate(_split3(acs * log2e)):
        pc_scr[n] = piece.astype(F32)
    for n, piece in enumerate(_split2(jnp.exp(acs)) + _split2(d_all * jnp.exp(total - acs))):
        pc_scr[3 + n] = piece.astype(F32)
    q_scr[...] = (acs - jnp.log(d_all)) * log2e
    ld_scr[...] = jnp.log(d_all + pltpu.roll(d_all, nr - SSD_HPG, 0)) * log2e

    rt_scr[...] = jnp.zeros(rt_scr.shape, F32)
    for c in range(nc):
        rows = slice(c * 16, (c + 1) * 16)
        for n in range(N_PIECES):
            rt_scr[16 * n:16 * (n + 1), :] = pc_scr[n, rows, :]
        col_scr[c] = rt_scr[...].T.astype(BF16)
        rq_scr[c, ROW_Q:ROW_Q + 16, :] = q_scr[rows, :]
        rq_scr[c, ROW_LD:ROW_LD + SSD_HPG, :] = ld_scr[c * 16:c * 16 + SSD_HPG, :]

    def fans(col, fan_ref):
        both = _dot(col, fan_ref[...])
        return both[:, 0:GROUP_CH], both[:, GROUP_CH:2 * GROUP_CH]

    if has_h0:
        hf_scr[...] = h0_ref[0, 0, 0, 0].T
        hb_scr[...] = h0_ref[0, 0, 1, 0].T
    else:
        hf_scr[...] = jnp.zeros((SSD_STATE, GROUP_CH), F32)
        hb_scr[...] = jnp.zeros((SSD_STATE, GROUP_CH), F32)
    xbd_scr[...] = jnp.zeros(xbd_scr.shape, BF16)

    n_half = GROUP_CH // 256

    def sweeps(i, carry):
        passes = ((nc - 1 - i, hb_scr, fanb_ref, yb_scr, 0), (i, hf_scr, fanf_ref, yf_scr, CHUNK - 1))
        staged = []
        for c, h_scr, fan_ref, y_scr, total_row in passes:
            r0 = pl.multiple_of(c * CHUNK, CHUNK)
            e_x, w_x = fans(col_scr[c], fan_ref)
            staged.append((r0, cbf_scr[pl.ds(r0, CHUNK), :], btf_scr[c], e_x, w_x))
        for hf in range(n_half):
            sl = slice(hf * 256, (hf + 1) * 256)
            for (c, h_scr, fan_ref, y_scr, total_row), (r0, cm, bt, e_x, w_x) in zip(passes, staged):
                e2 = e_x[:, sl]
                h = h_scr[:, sl]
                y_scr[pl.ds(r0, CHUNK), sl] = _dot(cm, h.astype(BF16)) * e2
                xw = (xs_scr[pl.ds(r0, CHUNK), sl] * w_x[:, sl]).astype(BF16)
                h_scr[:, sl] = h * e2[total_row:total_row + 1, :] + _dot(bt, xw)
        return carry

    lax.fori_loop(0, nc, sweeps, 0)

    def finish(j, carry):
        work = []
        for u in range(2):
            c = 2 * j + u
            r0 = pl.multiple_of(c * CHUNK, CHUNK)
            cb_mat = _dot_nt(cbf_scr[pl.ds(r0, CHUNK), :], bbf_scr[pl.ds(r0, CHUNK), :])
            acs_b = _dot(col_scr[c], bcast_ref[...])
            work.append((u, r0, acs_b, rq_scr[c], cb_mat))

        for u, r0, col, rq, cb_mat in work:
            for p in range(n_xblk):
                xb = xs_scr[pl.ds(r0, CHUNK), p * 128:(p + 1) * 128].astype(BF16)
                xbd_scr[u, p, 0:CHUNK, 0:SSD_HEADDIM] = xb[:, 0:SSD_HEADDIM]
                xbd_scr[u, p, CHUNK:2 * CHUNK, SSD_HEADDIM:2 * SSD_HEADDIM] = xb[:, SSD_HEADDIM:2 * SSD_HEADDIM]
        for p in range(n_xblk):
            for u, r0, col, rq, cb_mat in work:
                wpair = []
                for hh in (2 * p, 2 * p + 1):
                    pf = col[:, hh * CHUNK:(hh + 1) * CHUNK]
                    pb = col[:, (SSD_HPG + hh) * CHUNK:(SSD_HPG + hh + 1) * CHUNK]
                    qf = rq[ROW_Q + hh:ROW_Q + hh + 1, :]
                    qb = rq[ROW_Q + SSD_HPG + hh:ROW_Q + SSD_HPG + hh + 1, :]
                    ld = rq[ROW_LD + hh:ROW_LD + hh + 1, :]
                    arg = jnp.where(below, pf - qf, jnp.where(above, pb - qb, ld))
                    wpair.append((cb_mat * jnp.exp2(arg)).astype(BF16))
                yd_scr[u, :, p * 128:(p + 1) * 128] = _dot(jnp.concatenate(wpair, axis=1), xbd_scr[u, p])

        for u, r0, col, rq, cb_mat in work:
            ssq = jnp.zeros((CHUNK, 1), F32)
            for hf in range(n_half):
                sl = slice(hf * 256, (hf + 1) * 256)
                y = (yb_scr[pl.ds(r0, CHUNK), sl] + yf_scr[pl.ds(r0, CHUNK), sl] + yd_scr[u, :, sl]
                     + xs_scr[pl.ds(r0, CHUNK), sl] * dsk_ref[:, sl])
                yg = y * _silu(z_ref[0, pl.ds(r0, CHUNK), sl].astype(F32))
                yd_scr[u, :, sl] = yg
                ssq = ssq + jnp.sum(yg * yg, axis=1, keepdims=True)
            scale = lax.rsqrt(ssq * (1.0 / GROUP_CH) + EPS)
            for hf in range(n_half):
                sl = slice(hf * 256, (hf + 1) * 256)
                y_ref[0, pl.ds(r0, CHUNK), sl] = (yd_scr[u, :, sl] * scale * nw_ref[:, sl]).astype(BF16)
        return carry

    lax.fori_loop(0, nc // 2, finish, 0)

    if emit_state:
        for l in range(n_prev):
            st_ref[0, l] = prev_ref[0, l]
        st_ref[0, n_prev, 0, 0] = hf_scr[...].T
        st_ref[0, n_prev, 1, 0] = hb_scr[...].T


def _fan_matrix(first_head):
    m = np.zeros((CHUNK, 2 * GROUP_CH), np.float32)
    for part, bases in enumerate((COL_E, COL_W)):
        for base in bases:
            for j in range(SSD_HPG):
                m[base + first_head + j,
                  part * GROUP_CH + j * SSD_HEADDIM:part * GROUP_CH + (j + 1) * SSD_HEADDIM] = 1.0
    return jnp.asarray(m, BF16)


def _bcast_matrix():
    m = np.zeros((CHUNK, 2 * SSD_HPG * CHUNK), np.float32)
    for base in COL_ACS:
        for j in range(2 * SSD_HPG):
            m[base + j, j * CHUNK:(j + 1) * CHUNK] = 1.0
    return jnp.asarray(m, BF16)


def _ssd(proj, dtr, lw, h0, prev_state, *, batch, seq_len, layer, emit_state):
    L = seq_len
    proj3 = proj.reshape(batch, L, MAIN_COLS)
    has_h0 = h0 is not None
    gc = GROUP_CH
    W = gc + 2 * SSD_STATE

    in_specs = [
        pl.BlockSpec((1, L, gc), lambda b, g: (b, 0, COL_X // gc + g)),
        pl.BlockSpec((1, L, SSD_STATE), lambda b, g: (b, 0, COL_B // SSD_STATE + g)),
        pl.BlockSpec((1, L, SSD_STATE), lambda b, g: (b, 0, COL_C // SSD_STATE + g)),
        pl.BlockSpec((1, L, gc), lambda b, g: (b, 0, COL_Z // gc + g)),
        pl.BlockSpec((1, 16, L), lambda b, g: (g, 0, b)),
        pl.BlockSpec((SSD_CONV, gc), lambda b, g: (0, g)),
        pl.BlockSpec((SSD_CONV, SSD_STATE), lambda b, g: (0, SSD_INNER // SSD_STATE + g)),
        pl.BlockSpec((SSD_CONV, SSD_STATE), lambda b, g: (0, SSD_INNER // SSD_STATE + SSD_GROUPS + g)),
        pl.BlockSpec((1, gc), lambda b, g: (0, g)),
        pl.BlockSpec((1, SSD_STATE), lambda b, g: (0, SSD_INNER // SSD_STATE + g)),
        pl.BlockSpec((1, SSD_STATE), lambda b, g: (0, SSD_INNER // SSD_STATE + SSD_GROUPS + g)),
        pl.BlockSpec((1, 16, 2), lambda b, g: (g, 0, 0)),
        pl.BlockSpec((1, gc), lambda b, g: (0, g)),
        pl.BlockSpec((1, gc), lambda b, g: (0, g)),
        pl.BlockSpec((CHUNK, 2 * gc), lambda b, g: (0, 0)),
        pl.BlockSpec((CHUNK, 2 * gc), lambda b, g: (0, 0)),
        pl.BlockSpec((CHUNK, 2 * SSD_HPG * CHUNK), lambda b, g: (0, 0)),
    ]
    args = [proj3, proj3, proj3, proj3, dtr,
            lw["conv_w"], lw["conv_w"], lw["conv_w"], lw["conv_b"], lw["conv_b"], lw["conv_b"],
            lw["ssd_pcol"], lw["d_skip_x"], lw["ssd_norm_w"], _fan_matrix(0), _fan_matrix(SSD_HPG),
            _bcast_matrix()]
    if has_h0:
        in_specs.append(pl.BlockSpec((1, 1, 2, 1, gc, SSD_STATE), lambda b, g: (b, layer, 0, g, 0, 0)))
        args.append(h0)
    out_specs = [pl.BlockSpec((1, L, gc), lambda b, g: (b, 0, g))]
    out_shape = [jax.ShapeDtypeStruct((batch, L, SSD_INNER), BF16)]
    n_prev = 0 if prev_state is None else prev_state.shape[1]
    if n_prev:
        in_specs.append(pl.BlockSpec((1, n_prev, 2, 1, gc, SSD_STATE), lambda b, g: (b, 0, 0, g, 0, 0)))
        args.append(prev_state)
    if emit_state:
        out_specs.append(pl.BlockSpec((1, n_prev + 1, 2, 1, gc, SSD_STATE), lambda b, g: (b, 0, 0, g, 0, 0)))
        out_shape.append(jax.ShapeDtypeStruct((batch, n_prev + 1, 2, SSD_GROUPS, gc, SSD_STATE), F32))
    nc = L // CHUNK
    scratch = [
        pltpu.VMEM((L + 16, W), F32),
        pltpu.VMEM((L, gc), F32),
        pltpu.VMEM((L, SSD_STATE), BF16),
        pltpu.VMEM((L, SSD_STATE), BF16),
        pltpu.VMEM((nc, SSD_STATE, CHUNK), BF16),
        pltpu.VMEM((L, gc), F32),
        pltpu.VMEM((L, gc), F32),
        pltpu.VMEM((nc * 16, CHUNK), F32),
        pltpu.VMEM((N_PIECES, nc * 16, CHUNK), F32),
        pltpu.VMEM((nc * 16, CHUNK), F32),
        pltpu.VMEM((nc * 16, CHUNK), F32),
        pltpu.VMEM((CHUNK, CHUNK), F32),
        pltpu.VMEM((nc, CHUNK, CHUNK), BF16),
        pltpu.VMEM((nc, 24, CHUNK), F32),
        pltpu.VMEM((SSD_STATE, gc), F32),
        pltpu.VMEM((SSD_STATE, gc), F32),
        pltpu.VMEM((2, gc // 128, 2 * CHUNK, 128), BF16),
        pltpu.VMEM((2, CHUNK, gc), F32),
    ]
    outs = pl.pallas_call(
        functools.partial(_ssd_kernel, seq_len=L, has_h0=has_h0, emit_state=emit_state, n_prev=n_prev),
        grid=(batch, SSD_GROUPS),
        in_specs=in_specs,
        out_specs=out_specs,
        out_shape=out_shape,
        scratch_shapes=scratch,
        compiler_params=_cparams(("arbitrary", "arbitrary")),
        name="ssd",
    )(*args)
    y = outs[0].reshape(batch * L, SSD_INNER)
    return y, (outs[1] if emit_state else None)


def _route(sel, scores):
    neg = -jnp.inf

    def first_max(vals):
        m = vals[0]
        for v in vals[1:]:
            m = jnp.maximum(m, v)
        taken = jnp.zeros_like(m)
        flags = []
        for v in vals:
            f = jnp.where(v == m, 1.0, 0.0) * (1.0 - taken)
            flags.append(f)
            taken = taken + f
        return m, flags

    group_scores = []
    for j in range(N_EXPERT_GROUPS):
        a = sel[j * EXPERTS_PER_GROUP:(j + 1) * EXPERTS_PER_GROUP]
        m1, f1 = first_max(a)
        m2, _ = first_max([jnp.where(f > 0.5, neg, v) for f, v in zip(f1, a)])
        group_scores.append(m1 + m2)
    _, gflag = first_max(group_scores)
    masked = [jnp.where(gflag[e // EXPERTS_PER_GROUP] > 0.5, sel[e], neg) for e in range(N_EXPERTS)]
    _, f1 = first_max(masked)
    _, f2 = first_max([jnp.where(f > 0.5, neg, v) for f, v in zip(f1, masked)])
    w1 = sum(f * s for f, s in zip(f1, scores))
    w2 = sum(f * s for f, s in zip(f2, scores))
    tot = w1 + w2
    gates = [(f1[e] * w1 + f2[e] * w2) / tot for e in range(N_EXPERTS)]
    taken = jnp.zeros_like(tot)
    e_lo = e_hi = g_lo = g_hi = jnp.zeros_like(tot)
    for e in range(N_EXPERTS):
        sel_e = f1[e] + f2[e]
        low = sel_e * (1.0 - taken)
        high = sel_e - low
        taken = taken + sel_e
        e_lo = e_lo + float(e) * low
        e_hi = e_hi + float(e) * high
        g_lo = g_lo + low * gates[e]
        g_hi = g_hi + high * gates[e]
    group = sum(float(j) * gflag[j] for j in range(1, N_EXPERT_GROUPS))
    a = e_lo - EXPERTS_PER_GROUP * group
    b = e_hi - EXPERTS_PER_GROUP * group
    pair = a * (2 * EXPERTS_PER_GROUP - 1 - a) * 0.5 + (b - a - 1.0)
    bucket = group * PAIRS_PER_GROUP + pair
    return (g_lo, g_hi), bucket


def _mixout_kernel(*refs, has_pos, tm, rows_per_mod):
    it = iter(refs)
    x_ref = next(it)
    pos_ref = next(it) if has_pos else None
    ga_ref, gb_ref, u_ref, v_ref, yb_ref, mod_ref = (next(it) for _ in range(6))
    gnw_ref, ws_ref, bsx_ref, wa_ref, wb_ref, wo_ref, n2w_ref, rwt_ref, rb_ref = (next(it) for _ in range(9))
    rows_ref, gid_ref = (next(it) for _ in range(2))
    g_scr = next(it)

    sub = min(tm, MIX_SUB)
    spans = [slice(s * sub, (s + 1) * sub) for s in range(tm // sub)]
    n_chunks = tm // CHUNK
    gd = GMLP_WIDTH // GMLP_GROUPS

    u = [_gelu_tanh(u_ref[r, :].astype(F32)) for r in spans]
    vn = [(_rms(_gelu_tanh(v_ref[r, :].astype(F32))) * gnw_ref[...]).astype(BF16) for r in spans]

    def chunk_rows(c):
        s, local = divmod(c * CHUNK, sub)
        return s, slice(local, local + CHUNK)

    mixed_cols = [[None] * GMLP_GROUPS for _ in range(n_chunks)]
    for g in range(GMLP_GROUPS):
        pieces = []
        for c in range(n_chunks):
            s, rr = chunk_rows(c)
            pieces.append(vn[s][rr, g * gd:(g + 1) * gd])
        res = _dot(ws_ref[g], jnp.concatenate(pieces, axis=1))
        for c in range(n_chunks):
            mixed_cols[c][g] = res[:, c * gd:(c + 1) * gd]
    per_sub = sub // CHUNK
    ya = []
    for s in range(len(spans)):
        mixed = jnp.concatenate([jnp.concatenate(mixed_cols[s * per_sub + c], axis=1) + bsx_ref[...]
                                 for c in range(per_sub)], axis=0)
        ya.append((u[s] * mixed).astype(BF16))

    da = [_dot(y, wa_ref[...]) for y in ya]
    db = [_dot(yb_ref[r, :], wb_ref[...]) for r in spans]
    merged = [(jax.nn.sigmoid(ga_ref[r, :].astype(F32)) * a + jax.nn.sigmoid(gb_ref[r, :].astype(F32)) * b
               ).astype(BF16) for r, a, b in zip(spans, da, db)]
    do = [_dot(m, wo_ref[...]) for m in merged]

    h2s = []
    for r, o in zip(spans, do):
        x = x_ref[r, :]
        if has_pos:
            x = x + pos_ref[r, :]
        xn = x + mod_ref[0, 2:3, :] * o
        rows_ref[r, ROW_X:ROW_X + D_MODEL] = xn
        h2 = (_rms(xn) * n2w_ref[...]) * (1.0 + mod_ref[0, 4:5, :]) + mod_ref[0, 3:4, :]
        rows_ref[r, ROW_H:ROW_H + D_MODEL] = h2
        h2s.append(h2)

    r_hi, r_lo = _split2(rwt_ref[...])
    g_scr[...] = jnp.zeros(g_scr.shape, F32)
    mod_row = (pl.program_id(0) * tm) // rows_per_mod
    g_scr[AUX_MODROW:AUX_MODROW + 1, :] = jnp.full((1, tm), mod_row, jnp.int32).astype(F32)
    for r, h2 in zip(spans, h2s):
        h_hi, h_lo = _split2(h2)
        logits = _dot_nt(r_hi, h_hi) + _dot_nt(r_hi, h_lo) + _dot_nt(r_lo, h_hi)
        scores = jax.nn.sigmoid(logits)
        selm = scores + rb_ref[...]
        pair_gates, bucket = _route([selm[e:e + 1, :] for e in range(N_EXPERTS)],
                                    [scores[e:e + 1, :] for e in range(N_EXPERTS)])
        gid_ref[:, r] = bucket.astype(jnp.int32)
        for k in range(2):
            g_scr[k:k + 1, r] = pair_gates[k]
    rows_ref[:, ROW_AUX:ROW_W] = g_scr[...].T


def _mixout(x, pos, proj, yb, mod, lw, *, n_tokens):
    t, d = n_tokens, x.shape[1]
    nb = mod.shape[0]
    rows_per_mod = t // nb
    tm = _pick_tile(rows_per_mod, MIX_TILE)
    has_pos = pos is not None
    full = lambda shape: pl.BlockSpec(shape, lambda i: (0,) * len(shape), pipeline_mode=pl.Buffered(1))
    in_specs = [pl.BlockSpec((tm, d), lambda i: (i, 0))]
    args = [x]
    if has_pos:
        pos_blocks = pos.shape[0] // tm
        in_specs.append(pl.BlockSpec((tm, d), lambda i: (i % pos_blocks, 0)))
        args.append(pos)
    in_specs += [pl.BlockSpec((tm, d), lambda i: (i, 0)),
                 pl.BlockSpec((tm, d), lambda i: (i, 1)),
                 pl.BlockSpec((tm, d), lambda i: (i, 2)),
                 pl.BlockSpec((tm, d), lambda i: (i, 3)),
                 pl.BlockSpec((tm, SSD_INNER), lambda i: (i, 0)),
                 pl.BlockSpec((1, N_MOD, d), lambda i: ((i * tm) // rows_per_mod, 0, 0)),
                 full((1, GMLP_WIDTH)),
                 full((GMLP_GROUPS, CHUNK, CHUNK)),
                 full((CHUNK, GMLP_WIDTH)),
                 full((GMLP_WIDTH, d)),
                 full((SSD_INNER, d)),
                 full((d, d)),
                 full((1, d)),
                 full((N_EXPERTS, d)),
                 full((N_EXPERTS, 1))]
    args += [proj, proj, proj, proj, yb, mod,
             lw["gmlp_norm_w"], lw["gmlp_ws"], lw["gmlp_bs_x"], lw["w_branch_a"], lw["w_branch_b"], lw["w_out"],
             lw["norm2_w"], lw["router_wt"], lw["router_bias"]]
    return pl.pallas_call(
        functools.partial(_mixout_kernel, has_pos=has_pos, tm=tm, rows_per_mod=rows_per_mod),
        grid=(t // tm,),
        in_specs=in_specs,
        out_specs=[pl.BlockSpec((tm, ROW_W), lambda i: (i, 0)),
                   pl.BlockSpec((1, tm), lambda i: (0, i))],
        out_shape=[jax.ShapeDtypeStruct((t, ROW_W), F32),
                   jax.ShapeDtypeStruct((1, t), jnp.int32)],
        scratch_shapes=[pltpu.VMEM((ROW_W - ROW_AUX, tm), F32)],
        compiler_params=_cparams(("arbitrary",)),
        name="mixout",
    )(*args)


def _moe_plan(gid, n_tokens, tm):
    nb = N_BUCKETS
    t = n_tokens
    i32 = jnp.int32
    onehot = (gid[:, None] == jnp.arange(nb, dtype=i32)[None, :]).astype(i32)
    csum = jnp.cumsum(onehot, axis=0)
    rank = jnp.sum(onehot * (csum - 1), axis=1)
    counts = csum[-1]
    padded = ((counts + tm - 1) // tm) * tm
    ends = jnp.cumsum(padded)
    dest = jnp.sum(onehot * (ends - padded)[None, :], axis=1) + rank
    nt = pl.cdiv(t, tm) + nb
    tile_bucket = jnp.sum((jnp.arange(nt, dtype=i32) * tm)[:, None] >= ends[None, :], axis=1)
    tile_bucket = jnp.minimum(tile_bucket, nb - 1)
    pairs = [(a, b) for a in range(EXPERTS_PER_GROUP) for b in range(a + 1, EXPERTS_PER_GROUP)]
    lo = jnp.asarray([g * EXPERTS_PER_GROUP + a for g in range(N_EXPERT_GROUPS) for a, _ in pairs], i32)
    hi = jnp.asarray([g * EXPERTS_PER_GROUP + b for g in range(N_EXPERT_GROUPS) for _, b in pairs], i32)
    n_used = (ends[-1] // tm).astype(i32)
    tail = n_used + jnp.arange(nb, dtype=i32)
    cand = jnp.concatenate([jnp.where(padded > 0, ends - tm, -1), jnp.where(tail < nt, tail * tm, -1)])
    keep = cand >= 0
    order = jnp.argsort(jnp.logical_not(keep), stable=True)
    fill = jnp.concatenate([jnp.sum(keep).reshape(1), cand[order]]).astype(i32)
    return dest.astype(i32), lo[tile_bucket], hi[tile_bucket], n_used.reshape(1), fill


def _permute_kernel(*refs, tn, scatter, fill_rows):
    if fill_rows:
        fill_ref, idx_ref, src_ref, out_ref, zbuf, sem, zsem = refs
    else:
        idx_ref, src_ref, out_ref, sem = refs
    out_hbm = out_ref
    i = pl.program_id(0)

    if fill_rows:
        @pl.when(i == 0)
        def _():
            zbuf[...] = jnp.zeros(zbuf.shape, F32)
            n_fill = fill_ref[0]

            def start_fill(k, carry):
                start = pl.multiple_of(fill_ref[1 + k], fill_rows)
                pltpu.make_async_copy(zbuf, out_hbm.at[pl.ds(start, fill_rows), :], zsem).start()
                return carry

            def wait_fill(k, carry):
                pltpu.make_async_copy(zbuf, out_hbm.at[pl.ds(0, fill_rows), :], zsem).wait()
                return carry

            lax.fori_loop(0, n_fill, start_fill, 0)
            lax.fori_loop(0, n_fill, wait_fill, 0)

    group = 16

    def body(j, carry):
        r0 = pl.multiple_of(j * group, group)
        for k in range(group):
            p = idx_ref[0, 0, r0 + k]
            t8, s8 = j * (group // 8) + k // 8, k % 8
            if scatter:
                cp = pltpu.make_async_copy(src_ref.at[t8, pl.ds(s8, 1), :], out_ref.at[pl.ds(p, 1), :], sem)
            else:
                cp = pltpu.make_async_copy(src_ref.at[pl.ds(p, 1), :], out_ref.at[t8, pl.ds(s8, 1), :], sem)
            cp.start()
        return carry

    lax.fori_loop(0, tn // group, body, 0)
    block_ref = src_ref if scatter else out_ref
    pltpu.make_async_copy(block_ref, block_ref, sem).wait()


def _permute_rows(src, idx, *, n_rows, n_out, scatter, fill=None, fill_rows=0):
    w = src.shape[1]
    tn = _pick_tile(n_rows, 1024)
    steps = n_rows // tn
    idx3 = idx.reshape(steps, 1, tn)
    block = pl.BlockSpec((tn // 8, 8, w), lambda i, *_: (i, 0, 0))
    hbm = pl.BlockSpec(memory_space=pl.ANY)
    if scatter:
        src = src.reshape(src.shape[0] // 8, 8, w)
    in_specs = [pl.BlockSpec((1, 1, tn), lambda i, *_: (i, 0, 0), memory_space=pltpu.SMEM),
                block if scatter else hbm]
    scratch = [pltpu.SemaphoreType.DMA(())]
    args = [idx3, src]
    n_prefetch = 0
    if fill_rows:
        n_prefetch = 1
        args = [fill] + args
        scratch = [pltpu.VMEM((fill_rows, w), F32), pltpu.SemaphoreType.DMA(()), pltpu.SemaphoreType.DMA(())]
    out = pl.pallas_call(
        functools.partial(_permute_kernel, tn=tn, scatter=scatter, fill_rows=fill_rows),
        grid_spec=pltpu.PrefetchScalarGridSpec(
            num_scalar_prefetch=n_prefetch, grid=(steps,), in_specs=in_specs,
            out_specs=hbm if scatter else block, scratch_shapes=scratch),
        out_shape=jax.ShapeDtypeStruct((n_out, w) if scatter else (n_out // 8, 8, w), F32),
        compiler_params=_cparams(("arbitrary",)),
        name="permute_scatter" if scatter else "permute_gather",
    )(*args)
    return out if scatter else out.reshape(n_out, w)


def _moe_kernel(elo_ref, ehi_ref, nused_ref, xs_ref, g2_ref, wg0_ref, wu0_ref, wd0_ref, wg1_ref, wu1_ref, wd1_ref,
                fnw_ref, o_ref, *, tm, final, n_mod):
    used = pl.program_id(0) < nused_ref[0]

    @pl.when(jnp.logical_not(used))
    def _():
        o_ref[...] = jnp.zeros(o_ref.shape, F32)

    @pl.when(used)
    def _():
        rows = xs_ref[...]
        h = rows[:, ROW_H:ROW_H + D_MODEL].astype(BF16)
        y = None
        for k, (wg_ref, wu_ref, wd_ref) in enumerate(((wg0_ref, wu0_ref, wd0_ref), (wg1_ref, wu1_ref, wd1_ref))):
            act = _silu(_dot(h, wg_ref[0])) * _dot(h, wu_ref[0])
            gate = rows[:, ROW_AUX + k:ROW_AUX + k + 1]
            part = _dot((act * gate).astype(BF16), wd_ref[0])
            y = part if y is None else y + part
        if n_mod == 1:
            g2 = g2_ref[0:1, :]
        else:
            n = g2_ref.shape[0]
            mod_row = rows[:, ROW_AUX + AUX_MODROW:ROW_AUX + AUX_MODROW + 1]
            ids = lax.broadcasted_iota(jnp.int32, (tm, 2 * n), 1)
            ids = jnp.where(ids >= n, ids - n, ids).astype(F32)
            onehot = jnp.where(mod_row == ids, 1.0, 0.0).astype(BF16)
            g2 = _dot(onehot, jnp.concatenate(_split2(g2_ref[...]), axis=0))
        xn = rows[:, ROW_X:ROW_X + D_MODEL] + g2 * y
        if final:
            xn = _rms(xn) * fnw_ref[...]
        o_ref[...] = xn


def _moe(rows, gid, g2, lw, final_norm_w, *, n_tokens, final):
    t, d, tm = n_tokens, D_MODEL, MOE_TILE
    dest, tile_lo, tile_hi, n_used, fill = _moe_plan(gid.reshape(t), t, tm)
    nt = tile_lo.shape[0]
    n_sorted = nt * tm
    n_mod = g2.shape[0]
    if n_mod > 1:
        g2 = jnp.pad(g2, ((0, (-n_mod) % 8), (0, 0)))
    xs = _permute_rows(rows, dest, n_rows=t, n_out=n_sorted, scatter=True, fill=fill, fill_rows=tm)

    def tile(i, nu):
        return jnp.minimum(i, nu[0] - 1)

    def expert_specs(which):
        def idx(i, elo, ehi, nu):
            return ((elo, ehi)[which][tile(i, nu)], 0, 0)
        return [pl.BlockSpec((1, d, D_EXPERT), idx), pl.BlockSpec((1, d, D_EXPERT), idx),
                pl.BlockSpec((1, D_EXPERT, d), idx)]

    grid_spec = pltpu.PrefetchScalarGridSpec(
        num_scalar_prefetch=3,
        grid=(nt,),
        in_specs=[pl.BlockSpec((tm, ROW_W), lambda i, elo, ehi, nu: (tile(i, nu), 0)),
                  pl.BlockSpec(g2.shape, lambda i, elo, ehi, nu: (0, 0))]
                 + expert_specs(0) + expert_specs(1)
                 + [pl.BlockSpec((1, d), lambda i, elo, ehi, nu: (0, 0))],
        out_specs=pl.BlockSpec((tm, d), lambda i, elo, ehi, nu: (i, 0)))
    we = (lw["w_gate_e"], lw["w_up_e"], lw["w_down_e"])
    ys = pl.pallas_call(
        functools.partial(_moe_kernel, tm=tm, final=final, n_mod=n_mod),
        grid_spec=grid_spec,
        out_shape=jax.ShapeDtypeStruct((n_sorted, d), F32),
        compiler_params=_cparams(("arbitrary",)),
        name="moe",
    )(tile_lo, tile_hi, n_used, xs, g2, *we, *we, final_norm_w)
    return _permute_rows(ys, dest, n_rows=t, n_out=t, scatter=False)


def _grid_pos_embed(n_tokens, dim):
    rows = n_tokens // GRID_W
    quarter = dim // 4
    omega = 1.0 / (10000.0 ** (jnp.arange(quarter, dtype=F32) / quarter))
    r = jnp.arange(rows, dtype=F32)[:, None] * omega
    col = jnp.arange(GRID_W, dtype=F32)[:, None] * omega
    r_emb = jnp.concatenate([jnp.sin(r), jnp.cos(r)], axis=-1)
    c_emb = jnp.concatenate([jnp.sin(col), jnp.cos(col)], axis=-1)
    emb = jnp.concatenate([
        jnp.broadcast_to(r_emb[:, None, :], (rows, GRID_W, dim // 2)),
        jnp.broadcast_to(c_emb[None, :, :], (rows, GRID_W, dim // 2))], axis=-1)
    return emb.reshape(rows * GRID_W, dim)


def _dt_perm():
    return np.array([dr * SSD_HEADS + g * SSD_HPG + j
                     for g in range(SSD_GROUPS) for dr in range(2) for j in range(SSD_HPG)], np.int32)


def _layer_weights(l, w_in, norm1_w, gmlp_norm_w, gmlp_ws, gmlp_bs, conv_w, conv_b, dt_bias, a_log, d_skip,
                   ssd_norm_w, w_branch_a, w_branch_b, w_out, norm2_w, router_w, router_bias,
                   w_gate_e, w_up_e, w_down_e):
    perm = _dt_perm()
    w_dt = jnp.pad(w_in[l][:, MAIN_COLS:][:, perm], ((0, 0), (0, DT_PAD - 2 * SSD_HEADS)))
    prow = jnp.stack([dt_bias[l].reshape(-1)[perm], a_log[l].reshape(-1)[perm]], axis=0)
    prow = prow.reshape(2, SSD_GROUPS, 16).transpose(1, 0, 2)
    return {
        "norm1_w": norm1_w[l][None],
        "w_main": w_in[l][:, :MAIN_COLS].astype(BF16),
        "w_dt": w_dt.astype(BF16),
        "gmlp_norm_w": gmlp_norm_w[l][None],
        "gmlp_ws": gmlp_ws[l].astype(BF16),
        "gmlp_bs_x": jnp.repeat(gmlp_bs[l].T, GMLP_WIDTH // GMLP_GROUPS, axis=1),
        "conv_w": conv_w[l],
        "conv_b": conv_b[l][None],
        "ssd_pcol": prow.transpose(0, 2, 1),
        "d_skip_x": jnp.repeat(d_skip[l], SSD_HEADDIM)[None],
        "ssd_norm_w": ssd_norm_w[l][None],
        "w_branch_a": w_branch_a[l].astype(BF16),
        "w_branch_b": w_branch_b[l].astype(BF16),
        "w_out": w_out[l].astype(BF16),
        "norm2_w": norm2_w[l][None],
        "router_wt": router_w.T,
        "router_bias": router_bias[:, None],
        "w_gate_e": w_gate_e[l].astype(BF16),
        "w_up_e": w_up_e[l].astype(BF16),
        "w_down_e": w_down_e[l].astype(BF16),
    }


def _layer(x, pos, mod, lw, h0, prev_state, final_norm_w, *, batch, seq_len, layer, emit_state, final):
    t = batch * seq_len
    proj, dtr = _inproj(x, pos, mod, lw["norm1_w"], lw["w_main"], lw["w_dt"], n_tokens=t)
    yb, state = _ssd(proj, dtr, lw, h0, prev_state, batch=batch, seq_len=seq_len, layer=layer,
                     emit_state=emit_state)
    rows, gid = _mixout(x, pos, proj, yb, mod, lw, n_tokens=t)
    out = _moe(rows, gid, mod[:, N_MOD - 1, :], lw, final_norm_w, n_tokens=t, final=final)
    return out, state


def kernel(x_prompt, x_sample, state_ssd, c, c_ctx, w_mod, b_mod, norm1_w, w_in, gmlp_norm_w, gmlp_ws, gmlp_bs, conv_w, conv_b, dt_bias, a_log, d_skip, ssd_norm_w, w_branch_a, w_branch_b, w_out, norm2_w, router_w, router_bias, w_gate_e, w_up_e, w_down_e, final_norm_w):
    bp, lp, d = x_prompt.shape
    bs, ls, _ = x_sample.shape
    depth = w_mod.shape[0]
    assert 1 + bs <= MOD_ROWS and d == D_MODEL
    assert lp % (2 * CHUNK) == 0 and ls % (2 * CHUNK) == 0

    cond = jnp.concatenate([c_ctx[None], c, jnp.zeros((MOD_ROWS - 1 - bs, d), F32)], axis=0)
    mod = _modulation(cond, w_mod, b_mod).reshape(depth, MOD_ROWS, N_MOD, d)

    pos = _grid_pos_embed(ls, d)
    xp = x_prompt.reshape(bp * lp, d)
    xs = x_sample.reshape(bs * ls, d)
    cache = state_ssd.reshape(bs, depth, 2, SSD_GROUPS, GROUP_CH, SSD_STATE)
    fnw = final_norm_w[None]

    states = None
    for l in range(depth):
        lw = _layer_weights(l, w_in, norm1_w, gmlp_norm_w, gmlp_ws, gmlp_bs, conv_w, conv_b, dt_bias, a_log,
                            d_skip, ssd_norm_w, w_branch_a, w_branch_b, w_out, norm2_w, router_w, router_bias,
                            w_gate_e, w_up_e, w_down_e)
        final = l == depth - 1
        xp, states = _layer(xp, None, mod[l, 0:1], lw, None, states, fnw, batch=bp, seq_len=lp, layer=l,
                            emit_state=True, final=final)
        xs, _ = _layer(xs, pos if l == 0 else None, mod[l, 1:1 + bs], lw, cache, None, fnw, batch=bs, seq_len=ls,
                       layer=l, emit_state=False, final=final)

    new_state = states.reshape(bp, depth, 2, SSD_HEADS, SSD_HEADDIM, SSD_STATE)
    return (xp.reshape(bp, lp, d), xs.reshape(bs, ls, d), new_state.astype(x_prompt.dtype))
```
